```python
import math
import jax, jax.numpy as jnp
from jax import lax
import numpy as np

D_MODEL = 2048
BATCH = 4
SEQ = 8192
DEPTH = 4

GRID_W = 64
NA_HEADS = 8
NA_HEAD_DIM = 64
NA_WIDTH = NA_HEADS * NA_HEAD_DIM
NA_WIN_ROWS = 8
NA_WIN_COLS = 16
MLA_HEADS = 8
MLA_NOPE_DIM = 128
MLA_ROPE_DIM = 64
MLA_V_DIM = 128
MLA_QK_DIM = MLA_NOPE_DIM + MLA_ROPE_DIM
MLA_Q_RANK = 512
MLA_KV_RANK = 256
MLA_WIDTH = MLA_HEADS * MLA_V_DIM
ROPE_THETA = 10000.0
Q_BLOCK = 128
HY_CH = 512
HY_ORDER = 2
HY_POS_BANDS = 16
HY_POS_DIM = 1 + 2 * HY_POS_BANDS
HY_FILTER_HIDDEN = 64
HY_DECAY_TARGET = 1e-2
HY_DECAY_FAST = 0.3
HY_DECAY_SLOW = 1.5
HY_MAX_DECAY = math.log(HY_DECAY_TARGET) / HY_DECAY_FAST
HY_MIN_DECAY = math.log(HY_DECAY_TARGET) / HY_DECAY_SLOW
MIX_WIDTH = NA_WIDTH + MLA_WIDTH + HY_CH
IN_WIDTH = 3 * NA_WIDTH + MLA_Q_RANK + MLA_KV_RANK + MLA_ROPE_DIM + (HY_ORDER + 1) * HY_CH
IN_SPLITS = (3 * NA_WIDTH,
             3 * NA_WIDTH + MLA_Q_RANK,
             3 * NA_WIDTH + MLA_Q_RANK + MLA_KV_RANK,
             3 * NA_WIDTH + MLA_Q_RANK + MLA_KV_RANK + MLA_ROPE_DIM)
GROUP_SPLITS = (NA_WIDTH, NA_WIDTH + MLA_WIDTH)
D_FF = 5632
N_EXPERTS = 8
TOP_K = 2
N_DENSE = (DEPTH + 1) // 2
N_MOE = DEPTH // 2

EPS = 1e-6
NEG_INF = -1e30

kernel_name = 'hybrid_na_mla_hyena_moe_encoder'


def rmsnorm(x, g):
    xf = x.astype(jnp.float32)
    y = xf * lax.rsqrt(jnp.mean(xf * xf, axis=-1, keepdims=True) + EPS)
    return (y * g.astype(jnp.float32)).astype(x.dtype)


def swiglu(h, wg, wu, wd):
    return jnp.matmul(jax.nn.silu(jnp.matmul(h, wg)) * jnp.matmul(h, wu), wd)


def neighborhood_attention(q, k, v, rpb):
    B, L, H, Dh = q.shape
    rows = L // GRID_W
    kr = min(NA_WIN_ROWS, rows)
    kc = NA_WIN_COLS
    qg = q.reshape(B, rows, GRID_W, H, Dh)
    kg = k.reshape(B, rows, GRID_W, H, Dh)
    vg = v.reshape(B, rows, GRID_W, H, Dh)
    row_start = jnp.clip(jnp.arange(rows) - kr // 2, 0, rows - kr)
    cols = jnp.arange(GRID_W)
    col_start = jnp.clip(cols - kc // 2, 0, GRID_W - kc)
    col_mask = (cols[None, :] >= col_start[:, None]) & (cols[None, :] < col_start[:, None] + kc)
    col_off = jnp.clip(cols[None, :] - cols[:, None] + NA_WIN_COLS - 1, 0, 2 * NA_WIN_COLS - 2)
    scale = Dh ** -0.5

    def row_block(r):
        rs = row_start[r]
        q_blk = lax.dynamic_index_in_dim(qg, r, axis=1, keepdims=False)
        k_blk = lax.dynamic_slice_in_dim(kg, rs, kr, axis=1)
        v_blk = lax.dynamic_slice_in_dim(vg, rs, kr, axis=1)
        s = jnp.einsum('bqhd,bjkhd->bhqjk', q_blk, k_blk).astype(jnp.float32) * scale
        row_off = rs + jnp.arange(kr) - r + NA_WIN_ROWS - 1
        bias = rpb[:, row_off][:, :, col_off]
        s = s + jnp.transpose(bias, (0, 2, 1, 3))[None].astype(jnp.float32)
        s = jnp.where(col_mask[None, None, :, None, :], s, NEG_INF)
        p = jax.nn.softmax(s.reshape(B, H, GRID_W, kr * GRID_W), axis=-1)
        p = p.reshape(B, H, GRID_W, kr, GRID_W).astype(v.dtype)
        return jnp.einsum('bhqjk,bjkhd->bqhd', p, v_blk)

    o = lax.map(row_block, jnp.arange(rows))
    return jnp.transpose(o, (1, 0, 2, 3, 4)).reshape(B, L, H * Dh)


def rope(x, pos):
    half = x.shape[-1] // 2
    freqs = ROPE_THETA ** (-jnp.arange(half, dtype=jnp.float32) / half)
    ang = pos.astype(jnp.float32)[:, None] * freqs[None, :]
    cos = jnp.cos(ang)[:, None, :]
    sin = jnp.sin(ang)[:, None, :]
    xf = x.astype(jnp.float32)
    x1, x2 = xf[..., :half], xf[..., half:]
    return jnp.concatenate([x1 * cos - x2 * sin, x1 * sin + x2 * cos], axis=-1).astype(x.dtype)


def latent_attention(c_q, c_kv, k_rope_raw, q_norm_g, kv_norm_g, w_uq, w_ukv, q_head_g, k_head_g):
    B, L, _ = c_q.shape
    pos = jnp.arange(L)
    q = jnp.matmul(rmsnorm(c_q, q_norm_g), w_uq).reshape(B, L, MLA_HEADS, MLA_QK_DIM)
    kv = jnp.matmul(rmsnorm(c_kv, kv_norm_g), w_ukv).reshape(B, L, MLA_HEADS, MLA_NOPE_DIM + MLA_V_DIM)
    k_nope, v = kv[..., :MLA_NOPE_DIM], kv[..., MLA_NOPE_DIM:]
    k_rope = jnp.broadcast_to(k_rope_raw[:, :, None, :], (B, L, MLA_HEADS, MLA_ROPE_DIM))
    k = jnp.concatenate([k_nope, k_rope], axis=-1)
    q = rmsnorm(q, q_head_g)
    k = rmsnorm(k, k_head_g)
    q = jnp.concatenate([q[..., :MLA_NOPE_DIM], rope(q[..., MLA_NOPE_DIM:], pos)], axis=-1)
    k = jnp.concatenate([k[..., :MLA_NOPE_DIM], rope(k[..., MLA_NOPE_DIM:], pos)], axis=-1)
    nb = L // Q_BLOCK
    qb = jnp.transpose(q.reshape(B, nb, Q_BLOCK, MLA_HEADS, MLA_QK_DIM), (1, 0, 2, 3, 4))
    scale = MLA_QK_DIM ** -0.5

    def attend(q_blk):
        s = jnp.einsum('bqhd,bkhd->bhqk', q_blk, k).astype(jnp.float32) * scale
        p = jax.nn.softmax(s, axis=-1).astype(v.dtype)
        return jnp.einsum('bhqk,bkhd->bqhd', p, v)

    o = lax.map(attend, qb)
    return jnp.transpose(o, (1, 0, 2, 3, 4)).reshape(B, L, MLA_WIDTH)


def hyena_positional_features(L):
    t = jnp.linspace(0.0, 1.0, L, dtype=jnp.float32)[:, None]
    bands = jnp.linspace(1e-4, HY_POS_BANDS - 1, HY_POS_BANDS, dtype=jnp.float32)[None, :]
    w = 2.0 * math.pi * jnp.arange(L, dtype=jnp.float32)[:, None] / L
    return jnp.concatenate([t, jnp.cos(bands * w), -jnp.sin(bands * w)], axis=-1)


def hyena_two_sided_filters(L, w1, b1, w2, b2, w3, freq):
    z = hyena_positional_features(L)
    a = jnp.sin(freq[0].astype(jnp.float32) * (jnp.matmul(z, w1.astype(jnp.float32)) + b1.astype(jnp.float32)))
    a = jnp.sin(freq[1].astype(jnp.float32) * (jnp.matmul(a, w2.astype(jnp.float32)) + b2.astype(jnp.float32)))
    f = jnp.matmul(a, w3.astype(jnp.float32)).reshape(L, 2, HY_ORDER, HY_CH)
    t = jnp.linspace(0.0, 1.0, L, dtype=jnp.float32)
    deltas = jnp.abs(jnp.linspace(HY_MIN_DECAY, HY_MAX_DECAY, HY_CH, dtype=jnp.float32))
    window = jnp.exp(-t[:, None] * deltas[None, :])
    f = f * window[:, None, None, :]
    fwd = f[:, 0]
    bwd = f[1:, 1]
    k = jnp.concatenate([fwd, jnp.zeros((1, HY_ORDER, HY_CH), jnp.float32), bwd[::-1]], axis=0)
    return k * lax.rsqrt(jnp.sum(k * k, axis=0, keepdims=True) + EPS)


def fft_long_conv(u, k, bias):
    L = u.shape[1]
    uf = jnp.fft.rfft(u.astype(jnp.float32), n=2 * L, axis=1)
    kf = jnp.fft.rfft(k, n=2 * L, axis=0)
    y = jnp.fft.irfft(uf * kf[None], n=2 * L, axis=1)[:, :L]
    return (y + u.astype(jnp.float32) * bias.astype(jnp.float32)).astype(u.dtype)


def short_conv3(u, w, b):
    up = jnp.pad(u, ((0, 0), (1, 1), (0, 0)))
    return up[:, :-2] * w[0] + up[:, 1:-1] * w[1] + up[:, 2:] * w[2] + b


def hyena(u, short_w, short_b, w1, b1, w2, b2, w3, freq, bias):
    u = short_conv3(u, short_w, short_b)
    v, x1, x2 = u[..., :HY_CH], u[..., HY_CH:2 * HY_CH], u[..., 2 * HY_CH:]
    L = u.shape[1]
    k = hyena_two_sided_filters(L, w1, b1, w2, b2, w3, freq)
    z = x1 * fft_long_conv(v, k[:, 0], bias[0])
    z = x2 * fft_long_conv(z, k[:, 1], bias[1])
    return z


def moe_swiglu(h, w_router, w_gate, w_up, w_down):
    B, L, D = h.shape
    t = h.reshape(B * L, D)
    probs = jax.nn.softmax(jnp.matmul(t, w_router).astype(jnp.float32), axis=-1)
    top_p, top_i = lax.top_k(probs, TOP_K)
    top_p = top_p / jnp.sum(top_p, axis=-1, keepdims=True)
    combine = jnp.sum(jax.nn.one_hot(top_i, N_EXPERTS, dtype=jnp.float32) * top_p[..., None], axis=1)
    out = jnp.zeros_like(t)
    for e in range(N_EXPERTS):
        out = out + combine[:, e:e + 1].astype(t.dtype) * swiglu(t, w_gate[e], w_up[e], w_down[e])
    return out.reshape(B, L, D)


def setup_inputs(seed: int = 0) -> dict:
    key = jax.random.key(seed)
    ks = iter(jax.random.split(key, 40))

    def nrm(shape, scale):
        return jax.random.normal(next(ks), shape, jnp.float32) * scale

    def gain(shape):
        return 1.0 + nrm(shape, 0.02)

    return {
        'x': nrm((BATCH, SEQ, D_MODEL), 1.0),
        'attn_norm_g': gain((DEPTH, D_MODEL)),
        'w_in': nrm((DEPTH, D_MODEL, IN_WIDTH), D_MODEL ** -0.5),
        'na_q_g': gain((DEPTH, NA_HEAD_DIM)),
        'na_k_g': gain((DEPTH, NA_HEAD_DIM)),
        'na_rpb': nrm((DEPTH, NA_HEADS, 2 * NA_WIN_ROWS - 1, 2 * NA_WIN_COLS - 1), 0.1),
        'mla_q_norm_g': gain((DEPTH, MLA_Q_RANK)),
        'mla_kv_norm_g': gain((DEPTH, MLA_KV_RANK)),
        'mla_w_uq': nrm((DEPTH, MLA_Q_RANK, MLA_HEADS * MLA_QK_DIM), MLA_Q_RANK ** -0.5),
        'mla_w_ukv': nrm((DEPTH, MLA_KV_RANK, MLA_HEADS * (MLA_NOPE_DIM + MLA_V_DIM)), MLA_KV_RANK ** -0.5),
        'mla_q_g': gain((DEPTH, MLA_QK_DIM)),
        'mla_k_g': gain((DEPTH, MLA_QK_DIM)),
        'hy_short_w': nrm((DEPTH, 3, (HY_ORDER + 1) * HY_CH), 3 ** -0.5),
        'hy_short_b': nrm((DEPTH, (HY_ORDER + 1) * HY_CH), 0.02),
        'hy_w1': nrm((DEPTH, HY_POS_DIM, HY_FILTER_HIDDEN), HY_POS_DIM ** -0.5),
        'hy_b1': nrm((DEPTH, HY_FILTER_HIDDEN), 0.02),
        'hy_w2': nrm((DEPTH, HY_FILTER_HIDDEN, HY_FILTER_HIDDEN), HY_FILTER_HIDDEN ** -0.5),
        'hy_b2': nrm((DEPTH, HY_FILTER_HIDDEN), 0.02),
        'hy_w3': nrm((DEPTH, HY_FILTER_HIDDEN, 2 * HY_ORDER * HY_CH), HY_FILTER_HIDDEN ** -0.5),
        'hy_freq': gain((DEPTH, 2, HY_FILTER_HIDDEN)),
        'hy_bias': nrm((DEPTH, HY_ORDER, HY_CH), 0.1),
        'group_norm_g': gain((DEPTH, MIX_WIDTH)),
        'w_out': nrm((DEPTH, MIX_WIDTH, D_MODEL), MIX_WIDTH ** -0.5),
        'ffn_norm_g': gain((DEPTH, D_MODEL)),
        'dense_w_gate': nrm((N_DENSE, D_MODEL, D_FF), D_MODEL ** -0.5),
        'dense_w_up': nrm((N_DENSE, D_MODEL, D_FF), D_MODEL ** -0.5),
        'dense_w_down': nrm((N_DENSE, D_FF, D_MODEL), D_FF ** -0.5),
        'router_w': nrm((N_MOE, D_MODEL, N_EXPERTS), D_MODEL ** -0.5),
        'moe_w_gate': nrm((N_MOE, N_EXPERTS, D_MODEL, D_FF), D_MODEL ** -0.5),
        'moe_w_up': nrm((N_MOE, N_EXPERTS, D_MODEL, D_FF), D_MODEL ** -0.5),
        'moe_w_down': nrm((N_MOE, N_EXPERTS, D_FF, D_MODEL), D_FF ** -0.5),
    }


def reference(x, attn_norm_g, w_in, na_q_g, na_k_g, na_rpb, mla_q_norm_g, mla_kv_norm_g,
              mla_w_uq, mla_w_ukv, mla_q_g, mla_k_g, hy_short_w, hy_short_b, hy_w1, hy_b1,
              hy_w2, hy_b2, hy_w3, hy_freq, hy_bias, group_norm_g, w_out, ffn_norm_g,
              dense_w_gate, dense_w_up, dense_w_down, router_w, moe_w_gate, moe_w_up, moe_w_down):
    B, L, _ = x.shape
    for l in range(DEPTH):
        h = rmsnorm(x, attn_norm_g[l])
        proj = jnp.einsum('bld,de->ble', h, w_in[l])
        na_qkv, c_q, c_kv, k_rope, hy_u = jnp.split(proj, IN_SPLITS, axis=-1)
        na_q = rmsnorm(na_qkv[..., :NA_WIDTH].reshape(B, L, NA_HEADS, NA_HEAD_DIM), na_q_g[l])
        na_k = rmsnorm(na_qkv[..., NA_WIDTH:2 * NA_WIDTH].reshape(B, L, NA_HEADS, NA_HEAD_DIM), na_k_g[l])
        na_v = na_qkv[..., 2 * NA_WIDTH:].reshape(B, L, NA_HEADS, NA_HEAD_DIM)
        o_na = neighborhood_attention(na_q, na_k, na_v, na_rpb[l])
        o_mla = latent_attention(c_q, c_kv, k_rope, mla_q_norm_g[l], mla_kv_norm_g[l],
                                 mla_w_uq[l], mla_w_ukv[l], mla_q_g[l], mla_k_g[l])
        o_hy = hyena(hy_u, hy_short_w[l], hy_short_b[l], hy_w1[l], hy_b1[l], hy_w2[l],
                     hy_b2[l], hy_w3[l], hy_freq[l], hy_bias[l])
        g_na, g_mla, g_hy = jnp.split(group_norm_g[l], GROUP_SPLITS)
        mix = jnp.concatenate([rmsnorm(o_na, g_na), rmsnorm(o_mla, g_mla), rmsnorm(o_hy, g_hy)], axis=-1)
        x = x + jnp.einsum('ble,ed->bld', mix, w_out[l])
        h = rmsnorm(x, ffn_norm_g[l])
        if l % 2 == 0:
            i = l // 2
            x = x + swiglu(h, dense_w_gate[i], dense_w_up[i], dense_w_down[i])
        else:
            i = l // 2
            x = x + moe_swiglu(h, router_w[i], moe_w_gate[i], moe_w_up[i], moe_w_down[i])
    return x
```

```python
import functools
import math

import jax
import jax.numpy as jnp
import numpy as np
from jax import lax
from jax.experimental import pallas as pl
from jax.experimental.pallas import tpu as pltpu

F32 = jnp.float32
BF16 = jnp.bfloat16

GRID_W = 64
NA_HEADS = 8
NA_HEAD_DIM = 64
NA_WIDTH = NA_HEADS * NA_HEAD_DIM
NA_WIN_ROWS = 8
NA_WIN_COLS = 16
MLA_HEADS = 8
MLA_NOPE_DIM = 128
MLA_ROPE_DIM = 64
MLA_V_DIM = 128
MLA_QK_DIM = MLA_NOPE_DIM + MLA_ROPE_DIM
MLA_Q_RANK = 512
MLA_KV_RANK = 256
ROPE_THETA = 10000.0
HY_CH = 512
HY_ORDER = 2
HY_POS_BANDS = 16
HY_DECAY_TARGET = 1e-2
HY_DECAY_FAST = 0.3
HY_DECAY_SLOW = 1.5
HY_MAX_DECAY = math.log(HY_DECAY_TARGET) / HY_DECAY_FAST
HY_MIN_DECAY = math.log(HY_DECAY_TARGET) / HY_DECAY_SLOW
N_EXPERTS = 8
TOP_K = 2
EPS = 1e-6
NEG_INF = -1e30

COL_NA_Q = 0
COL_NA_K = NA_WIDTH
COL_NA_V = 2 * NA_WIDTH
COL_CQ = 3 * NA_WIDTH
COL_CKV = COL_CQ + MLA_Q_RANK
COL_KROPE = COL_CKV + MLA_KV_RANK
COL_HY = COL_KROPE + MLA_ROPE_DIM
IN_WIDTH = COL_HY + (HY_ORDER + 1) * HY_CH
LANES = 128
IN_WIDTH_PAD = 4096

VMEM_LIMIT_BYTES = 56 * 1024 * 1024

NA_Q_ROWS = 8
NA_K_ROWS = 16


def _params(*sem):
    return pltpu.CompilerParams(dimension_semantics=sem, vmem_limit_bytes=VMEM_LIMIT_BYTES)


def _add_norm_kernel(*refs, n_delta, with_router, with_norm):
    x_ref = refs[0]
    d_refs = refs[1:1 + n_delta]
    pos = 1 + n_delta
    x = x_ref[...]
    for d in d_refs:
        x = x + d[...].astype(F32)
    if not with_norm:
        refs[pos][...] = x
        return
    g_ref = refs[pos]
    pos += 1
    if with_router:
        wr_ref = refs[pos]
        pos += 1
    xo_ref, h_ref = refs[pos], refs[pos + 1]
    xo_ref[...] = x
    h = x * lax.rsqrt(jnp.mean(x * x, axis=-1, keepdims=True) + EPS) * g_ref[...]
    h_ref[...] = h.astype(BF16)
    if with_router:
        refs[pos + 2][...] = jnp.dot(h, wr_ref[...], preferred_element_type=F32,
                                     precision=lax.Precision.HIGHEST)


def _add_norm(x, deltas, gain=None, router_w=None):
    T, D = x.shape
    tm = min(512, T)
    with_norm = gain is not None
    with_router = router_w is not None
    row = pl.BlockSpec((tm, D), lambda i: (i, 0))
    in_specs = [row] + [row] * len(deltas)
    args = [x] + list(deltas)
    out_shape = [jax.ShapeDtypeStruct((T, D), F32)]
    out_specs = [row]
    if with_norm:
        in_specs.append(pl.BlockSpec((1, D), lambda i: (0, 0)))
        args.append(gain.reshape(1, D))
        if with_router:
            in_specs.append(pl.BlockSpec((D, LANES), lambda i: (0, 0)))
            args.append(router_w)
        out_shape.append(jax.ShapeDtypeStruct((T, D), BF16))
        out_specs.append(row)
        if with_router:
            out_shape.append(jax.ShapeDtypeStruct((T, LANES), F32))
            out_specs.append(pl.BlockSpec((tm, LANES), lambda i: (i, 0)))
    out = pl.pallas_call(
        functools.partial(_add_norm_kernel, n_delta=len(deltas), with_router=with_router,
                          with_norm=with_norm),
        grid=(T // tm,), in_specs=in_specs, out_specs=out_specs, out_shape=out_shape,
        compiler_params=_params("parallel"), name="add_norm")(*args)
    return out


def _mm_kernel(a_ref, b_ref, o_ref):
    o_ref[...] = jnp.dot(a_ref[...], b_ref[...], preferred_element_type=F32).astype(o_ref.dtype)


def _matmul(a, b, out_dtype, tn):
    M, K = a.shape
    N = b.shape[1]
    tm = min(1024, M)
    return pl.pallas_call(
        _mm_kernel, grid=(M // tm, N // tn),
        in_specs=[pl.BlockSpec((tm, K), lambda i, j: (i, 0)),
                  pl.BlockSpec((K, tn), lambda i, j: (0, j))],
        out_specs=pl.BlockSpec((tm, tn), lambda i, j: (i, j)),
        out_shape=jax.ShapeDtypeStruct((M, N), out_dtype),
        compiler_params=_params("parallel", "parallel"), name="matmul")(a, b)


def _na_bias_table(rpb, rows):
    n_groups = rows // NA_Q_ROWS
    reps = (0, min(1, n_groups - 1), n_groups - 1)
    row_off = np.zeros((3, NA_Q_ROWS, NA_K_ROWS), np.int32)
    row_ok = np.zeros((3, NA_Q_ROWS, NA_K_ROWS), bool)
    for p, g in enumerate(reps):
        start = int(np.clip(g * NA_Q_ROWS - NA_WIN_ROWS // 2, 0, rows - NA_K_ROWS))
        for i in range(NA_Q_ROWS):
            r = g * NA_Q_ROWS + i
            rs = int(np.clip(r - NA_WIN_ROWS // 2, 0, rows - NA_WIN_ROWS))
            for j in range(NA_K_ROWS):
                kr = start + j
                ok = rs <= kr < rs + NA_WIN_ROWS
                row_ok[p, i, j] = ok
                row_off[p, i, j] = np.clip(kr - r + NA_WIN_ROWS - 1, 0, 2 * NA_WIN_ROWS - 2)
    cols = np.arange(GRID_W)
    col_start = np.clip(cols - NA_WIN_COLS // 2, 0, GRID_W - NA_WIN_COLS)
    col_ok = (cols[None, :] >= col_start[:, None]) & (cols[None, :] < col_start[:, None] + NA_WIN_COLS)
    col_off = np.clip(cols[None, :] - cols[:, None] + NA_WIN_COLS - 1, 0, 2 * NA_WIN_COLS - 2)
    tab = rpb[:, row_off[:, :, None, :, None], col_off[None, None, :, None, :]]
    ok = row_ok[:, :, None, :, None] & col_ok[None, None, :, None, :]
    tab = jnp.where(ok[None], tab.astype(F32), NEG_INF)
    return tab.reshape(NA_HEADS, 3, NA_Q_ROWS * GRID_W, NA_K_ROWS * GRID_W)


def _na_kernel(q_ref, k_ref, v_ref, bias_ref, gq_ref, gk_ref, ones_ref, o_ref, *, rows):
    g = pl.program_id(2)
    tq = NA_Q_ROWS * GRID_W
    tk = NA_K_ROWS * GRID_W
    start = jnp.clip(g * NA_Q_ROWS - NA_WIN_ROWS // 2, 0, rows - NA_K_ROWS) * GRID_W
    start = pl.multiple_of(start, GRID_W)
    ones_bd = ones_ref[...]

    def head_norm(x, gain):
        x2 = x * x
        hi = x2.astype(BF16)
        lo = (x2 - hi.astype(F32)).astype(BF16)
        ssq = (jnp.dot(hi, ones_bd, preferred_element_type=F32)
               + jnp.dot(lo, ones_bd, preferred_element_type=F32))
        return x * lax.rsqrt(ssq * (1.0 / NA_HEAD_DIM) + EPS) * gain

    q = head_norm(q_ref[0].astype(F32), gq_ref[...]) * (NA_HEAD_DIM ** -0.5)
    kw = head_norm(k_ref[0, pl.ds(start, tk), :].astype(F32), gk_ref[...]).astype(BF16)
    vw = v_ref[0, pl.ds(start, tk), :]
    lane = lax.broadcasted_iota(jnp.int32, (tq, LANES), 1)
    outs = []
    for hh in range(2):
        sel = (lane < NA_HEAD_DIM) if hh == 0 else (lane >= NA_HEAD_DIM)
        qm = jnp.where(sel, q, 0.0).astype(BF16)
        s = lax.dot_general(qm, kw, (((1,), (1,)), ((), ())), preferred_element_type=F32)
        s = s + bias_ref[hh, 0]
        m = jnp.max(s, axis=-1, keepdims=True)
        p = jnp.exp(s - m)
        l = jnp.sum(p, axis=-1, keepdims=True)
        o = jnp.dot(p.astype(BF16), vw, preferred_element_type=F32)
        outs.append(o / l)
    o_ref[0] = jnp.where(lane < NA_HEAD_DIM, outs[0], outs[1]).astype(o_ref.dtype)


def _neighborhood_attention(proj3, rpb, q_g, k_g):
    B, L, _ = proj3.shape
    rows = L // GRID_W
    n_groups = rows // NA_Q_ROWS
    tq = NA_Q_ROWS * GRID_W
    tk = NA_K_ROWS * GRID_W
    bias = _na_bias_table(rpb, rows)
    gq = jnp.tile(q_g.astype(F32), 2).reshape(1, LANES)
    gk = jnp.tile(k_g.astype(F32), 2).reshape(1, LANES)
    ones_bd = jnp.asarray(np.kron(np.eye(2), np.ones((NA_HEAD_DIM, NA_HEAD_DIM))), BF16)
    qb, kb, vb = COL_NA_Q // LANES, COL_NA_K // LANES, COL_NA_V // LANES

    def pattern(g):
        return (g > 0).astype(jnp.int32) + (g == n_groups - 1).astype(jnp.int32)

    return pl.pallas_call(
        functools.partial(_na_kernel, rows=rows),
        grid=(NA_HEADS // 2, B, n_groups),
        in_specs=[
            pl.BlockSpec((1, tq, LANES), lambda hp, b, g: (b, g, qb + hp)),
            pl.BlockSpec((1, L, LANES), lambda hp, b, g: (b, 0, kb + hp)),
            pl.BlockSpec((1, L, LANES), lambda hp, b, g: (b, 0, vb + hp)),
            pl.BlockSpec((2, 1, tq, tk), lambda hp, b, g: (hp, pattern(g), 0, 0)),
            pl.BlockSpec((1, LANES), lambda hp, b, g: (0, 0)),
            pl.BlockSpec((1, LANES), lambda hp, b, g: (0, 0)),
            pl.BlockSpec((LANES, LANES), lambda hp, b, g: (0, 0)),
        ],
        out_specs=pl.BlockSpec((1, tq, LANES), lambda hp, b, g: (b, g, hp)),
        out_shape=jax.ShapeDtypeStruct((B, L, NA_WIDTH), BF16),
        compiler_params=_params("parallel", "parallel", "arbitrary"),
        name="neighborhood_attention")(proj3, proj3, proj3, bias, gq, gk, ones_bd)


def _mla_prep_kernel(cq_ref, ckv_ref, krope_ref, gqn_ref, gkvn_ref, wuq_ref, wukv_ref,
                     qgn_ref, qgr_ref, kgn_ref, kgr_ref, cos_ref, sin_ref,
                     qn_ref, qr_ref, kn_ref, kr_ref, v_ref):
    def rms(x, g):
        return x * lax.rsqrt(jnp.mean(x * x, axis=-1, keepdims=True) + EPS) * g

    tm = cq_ref.shape[0]
    nope_w = MLA_HEADS * MLA_NOPE_DIM
    cq = rms(cq_ref[...].astype(F32), gqn_ref[...]).astype(BF16)
    q = jnp.dot(cq, wuq_ref[...], preferred_element_type=F32)
    ckv = rms(ckv_ref[...].astype(F32), gkvn_ref[...]).astype(BF16)
    kv = jnp.dot(ckv, wukv_ref[...], preferred_element_type=F32)
    v_ref[...] = kv[:, nope_w:].astype(BF16)

    lane = lax.broadcasted_iota(jnp.int32, (tm, LANES), 1)
    first = lane < MLA_ROPE_DIM
    low_half = (lane % MLA_ROPE_DIM) < (MLA_ROPE_DIM // 2)
    cosv = cos_ref[...]
    sinv = sin_ref[...]
    scale = MLA_QK_DIM ** -0.5
    inv_dim = 1.0 / MLA_QK_DIM

    def rope(x):
        swapped = jnp.where(low_half, pltpu.roll(x, LANES - MLA_ROPE_DIM // 2, 1),
                            pltpu.roll(x, MLA_ROPE_DIM // 2, 1))
        return x * cosv + swapped * sinv

    qgn, qgr = qgn_ref[...], qgr_ref[...]
    kgn, kgr = kgn_ref[...], kgr_ref[...]
    kraw = krope_ref[...].astype(F32)
    kdup = jnp.where(first, kraw, pltpu.roll(kraw, MLA_ROPE_DIM, 1))
    k_rope_sq = jnp.where(first, kraw * kraw, 0.0)
    k_roped = rope(kdup * kgr)

    for j in range(MLA_HEADS // 2):
        qr_j = q[:, nope_w + LANES * j:nope_w + LANES * (j + 1)]
        qr_sq = qr_j * qr_j
        q_rs, k_rs = [], []
        for hh in range(2):
            h = 2 * j + hh
            cols = slice(LANES * h, LANES * (h + 1))
            qn_h = q[:, cols]
            t = qn_h * qn_h + jnp.where(first if hh == 0 else jnp.logical_not(first), qr_sq, 0.0)
            rs = lax.rsqrt(jnp.sum(t, axis=-1, keepdims=True) * inv_dim + EPS)
            qn_ref[:, cols] = (qn_h * rs * qgn * scale).astype(BF16)
            q_rs.append(rs)
            kn_h = kv[:, cols]
            t = kn_h * kn_h + k_rope_sq
            rs = lax.rsqrt(jnp.sum(t, axis=-1, keepdims=True) * inv_dim + EPS)
            kn_ref[:, cols] = (kn_h * rs * kgn).astype(BF16)
            k_rs.append(rs)
        pair = slice(LANES * j, LANES * (j + 1))
        qr_ref[:, pair] = (rope(qr_j * jnp.where(first, q_rs[0], q_rs[1]) * qgr) * scale).astype(BF16)
        kr_ref[:, pair] = (k_roped * jnp.where(first, k_rs[0], k_rs[1])).astype(BF16)


def _mla_prep(proj, L, q_norm_g, kv_norm_g, w_uq, w_ukv, q_head_g, k_head_g):
    T = proj.shape[0]
    tm = min(512, L)
    H = MLA_HEADS
    wq = w_uq.reshape(MLA_Q_RANK, H, MLA_QK_DIM)
    wq = jnp.concatenate([wq[:, :, :MLA_NOPE_DIM].reshape(MLA_Q_RANK, -1),
                          wq[:, :, MLA_NOPE_DIM:].reshape(MLA_Q_RANK, -1)], axis=1).astype(BF16)
    wkv = w_ukv.reshape(MLA_KV_RANK, H, MLA_NOPE_DIM + MLA_V_DIM)
    wkv = jnp.concatenate([wkv[:, :, :MLA_NOPE_DIM].reshape(MLA_KV_RANK, -1),
                           wkv[:, :, MLA_NOPE_DIM:].reshape(MLA_KV_RANK, -1)], axis=1).astype(BF16)
    half = MLA_ROPE_DIM // 2
    freqs = ROPE_THETA ** (-jnp.arange(half, dtype=F32) / half)
    ang = jnp.arange(L).astype(F32)[:, None] * freqs[None, :]
    cos, sin = jnp.cos(ang), jnp.sin(ang)
    cos2 = jnp.tile(jnp.concatenate([cos, cos], axis=1), (1, 2))
    sin2 = jnp.tile(jnp.concatenate([-sin, sin], axis=1), (1, 2))

    def vec(g, reps=1):
        return jnp.tile(g.astype(F32), reps).reshape(1, -1)

    const = lambda shape: pl.BlockSpec(shape, lambda i: (0, 0))
    nblk = L // tm
    outs = pl.pallas_call(
        _mla_prep_kernel, grid=(T // tm,),
        in_specs=[
            pl.BlockSpec((tm, MLA_Q_RANK), lambda i: (i, COL_CQ // MLA_Q_RANK)),
            pl.BlockSpec((tm, MLA_KV_RANK), lambda i: (i, COL_CKV // MLA_KV_RANK)),
            pl.BlockSpec((tm, LANES), lambda i: (i, COL_KROPE // LANES)),
            const((1, MLA_Q_RANK)), const((1, MLA_KV_RANK)),
            const(wq.shape), const(wkv.shape),
            const((1, LANES)), const((1, LANES)), const((1, LANES)), const((1, LANES)),
            pl.BlockSpec((tm, LANES), lambda i: (i % nblk, 0)),
            pl.BlockSpec((tm, LANES), lambda i: (i % nblk, 0)),
        ],
        out_specs=[
            pl.BlockSpec((tm, H * MLA_NOPE_DIM), lambda i: (i, 0)),
            pl.BlockSpec((tm, H * MLA_ROPE_DIM), lambda i: (i, 0)),
            pl.BlockSpec((tm, H * MLA_NOPE_DIM), lambda i: (i, 0)),
            pl.BlockSpec((tm, H * MLA_ROPE_DIM), lambda i: (i, 0)),
            pl.BlockSpec((tm, H * MLA_V_DIM), lambda i: (i, 0)),
        ],
        out_shape=[
            jax.ShapeDtypeStruct((T, H * MLA_NOPE_DIM), BF16),
            jax.ShapeDtypeStruct((T, H * MLA_ROPE_DIM), BF16),
            jax.ShapeDtypeStruct((T, H * MLA_NOPE_DIM), BF16),
            jax.ShapeDtypeStruct((T, H * MLA_ROPE_DIM), BF16),
            jax.ShapeDtypeStruct((T, H * MLA_V_DIM), BF16),
        ],
        compiler_params=_params("parallel"), name="mla_prep")(
            proj, proj, proj, vec(q_norm_g), vec(kv_norm_g), wq, wkv,
            vec(q_head_g[:MLA_NOPE_DIM]), vec(q_head_g[MLA_NOPE_DIM:], 2),
            vec(k_head_g[:MLA_NOPE_DIM]), vec(k_head_g[MLA_NOPE_DIM:], 2), cos2, sin2)
    return outs


def _flash_kernel(qn_ref, qr_ref, kn_ref, kr_ref, v_ref, o_ref, q_sc, m_sc, l_sc, acc_sc):
    h = pl.program_id(1)
    ki = pl.program_id(3)
    tq = qn_ref.shape[1]

    @pl.when(ki == 0)
    def _():
        lane = lax.broadcasted_iota(jnp.int32, (tq, LANES), 1)
        own = (lane // MLA_ROPE_DIM) == (h % 2)
        q_sc[:, :LANES] = qn_ref[0]
        q_sc[:, LANES:] = jnp.where(own, qr_ref[0], jnp.zeros_like(qr_ref[0]))
        m_sc[...] = jnp.full_like(m_sc, NEG_INF)
        l_sc[...] = jnp.zeros_like(l_sc)
        acc_sc[...] = jnp.zeros_like(acc_sc)

    k = jnp.concatenate([kn_ref[0], kr_ref[0]], axis=-1)
    s = lax.dot_general(q_sc[...], k, (((1,), (1,)), ((), ())), preferred_element_type=F32)
    m_prev = m_sc[...]
    m_new = jnp.maximum(m_prev, jnp.max(s, axis=-1, keepdims=True))
    alpha = jnp.exp(m_prev - m_new)
    p = jnp.exp(s - m_new)
    l_sc[...] = alpha * l_sc[...] + jnp.sum(p, axis=-1, keepdims=True)
    acc_sc[...] = alpha * acc_sc[...] + jnp.dot(p.astype(BF16), v_ref[0], preferred_element_type=F32)
    m_sc[...] = m_new

    @pl.when(ki == pl.num_programs(3) - 1)
    def _():
        o_ref[0] = (acc_sc[...] / l_sc[...]).astype(o_ref.dtype)


def _flash_attention(qn, qr, kn, kr, v, B, L):
    H = MLA_HEADS
    tq = min(1024, L)
    tk = min(1024, L)
    r3 = lambda a: a.reshape(B, L, a.shape[-1])
    return pl.pallas_call(
        _flash_kernel, grid=(B, H, L // tq, L // tk),
        in_specs=[
            pl.BlockSpec((1, tq, LANES), lambda b, h, qi, ki: (b, qi, h)),
            pl.BlockSpec((1, tq, LANES), lambda b, h, qi, ki: (b, qi, h // 2)),
            pl.BlockSpec((1, tk, LANES), lambda b, h, qi, ki: (b, ki, h)),
            pl.BlockSpec((1, tk, LANES), lambda b, h, qi, ki: (b, ki, h // 2)),
            pl.BlockSpec((1, tk, LANES), lambda b, h, qi, ki: (b, ki, h)),
        ],
        out_specs=pl.BlockSpec((1, tq, LANES), lambda b, h, qi, ki: (b, qi, h)),
        out_shape=jax.ShapeDtypeStruct((B, L, H * MLA_V_DIM), BF16),
        scratch_shapes=[pltpu.VMEM((tq, 2 * LANES), BF16), pltpu.VMEM((tq, 1), F32),
                        pltpu.VMEM((tq, 1), F32), pltpu.VMEM((tq, LANES), F32)],
        compiler_params=_params("parallel", "parallel", "parallel", "arbitrary"),
        name="mla_flash")(r3(qn), r3(qr), r3(kn), r3(kr), r3(v))


def _hyena_filters(L, w1, b1, w2, b2, w3, freq):
    t = jnp.linspace(0.0, 1.0, L, dtype=F32)[:, None]
    bands = jnp.linspace(1e-4, HY_POS_BANDS - 1, HY_POS_BANDS, dtype=F32)[None, :]
    w = 2.0 * math.pi * jnp.arange(L, dtype=F32)[:, None] / L
    z = jnp.concatenate([t, jnp.cos(bands * w), -jnp.sin(bands * w)], axis=-1)
    hp = lax.Precision.HIGHEST
    a = jnp.sin(freq[0] * (jnp.matmul(z, w1, precision=hp) + b1))
    a = jnp.sin(freq[1] * (jnp.matmul(a, w2, precision=hp) + b2))
    f = jnp.matmul(a, w3, precision=hp).reshape(L, 2, HY_ORDER, HY_CH)
    deltas = jnp.abs(jnp.linspace(HY_MIN_DECAY, HY_MAX_DECAY, HY_CH, dtype=F32))
    f = f * jnp.exp(-t * deltas[None, :])[:, None, None, :]
    k = jnp.concatenate([f[:, 0], jnp.zeros((1, HY_ORDER, HY_CH), F32), f[1:, 1][::-1]], axis=0)
    return k * lax.rsqrt(jnp.sum(k * k, axis=0, keepdims=True) + EPS)


def _fft_long_conv(u, k, bias):
    L = u.shape[1]
    uf = jnp.fft.rfft(u, n=2 * L, axis=1)
    kf = jnp.fft.rfft(k, n=2 * L, axis=0)
    y = jnp.fft.irfft(uf * kf[None], n=2 * L, axis=1)[:, :L]
    return y + u * bias


def _hyena(u, short_w, short_b, w1, b1, w2, b2, w3, freq, bias):
    up = jnp.pad(u, ((0, 0), (1, 1), (0, 0)))
    u = up[:, :-2] * short_w[0] + up[:, 1:-1] * short_w[1] + up[:, 2:] * short_w[2] + short_b
    v, x1, x2 = u[..., :HY_CH], u[..., HY_CH:2 * HY_CH], u[..., 2 * HY_CH:]
    k = _hyena_filters(u.shape[1], w1, b1, w2, b2, w3, freq)
    z = x1 * _fft_long_conv(v, k[:, 0], bias[0])
    return x2 * _fft_long_conv(z, k[:, 1], bias[1])


def _mix_norm_kernel(na_ref, mla_ref, hy_ref, g_ref, o_ref):
    def rms(x, g):
        return (x * lax.rsqrt(jnp.mean(x * x, axis=-1, keepdims=True) + EPS) * g).astype(BF16)

    a, b = NA_WIDTH, NA_WIDTH + MLA_HEADS * MLA_V_DIM
    o_ref[:, :a] = rms(na_ref[...].astype(F32), g_ref[:, :a])
    o_ref[:, a:b] = rms(mla_ref[...].astype(F32), g_ref[:, a:b])
    o_ref[:, b:] = rms(hy_ref[...].astype(F32), g_ref[:, b:])


def _mix_norm(o_na, o_mla, o_hy, gain):
    T = o_na.shape[0]
    W = o_na.shape[1] + o_mla.shape[1] + o_hy.shape[1]
    tm = min(1024, T)
    row = lambda a: pl.BlockSpec((tm, a.shape[1]), lambda i: (i, 0))
    return pl.pallas_call(
        _mix_norm_kernel, grid=(T // tm,),
        in_specs=[row(o_na), row(o_mla), row(o_hy), pl.BlockSpec((1, W), lambda i: (0, 0))],
        out_specs=pl.BlockSpec((tm, W), lambda i: (i, 0)),
        out_shape=jax.ShapeDtypeStruct((T, W), BF16),
        compiler_params=_params("parallel"), name="mix_norm")(
            o_na, o_mla, o_hy, gain.astype(F32).reshape(1, W))


def _ffn_kernel(te_ref, na_ref, h_ref, wg_ref, wu_ref, wd_ref, rw_ref, o_ref, acc_ref):
    i = pl.program_id(0)
    j = pl.program_id(1)
    last = pl.num_programs(1) - 1
    active = i < na_ref[0]

    @pl.when(jnp.logical_and(active, j == 0))
    def _():
        acc_ref[...] = jnp.zeros_like(acc_ref)

    @pl.when(active)
    def _():
        h = h_ref[...]
        g = jnp.dot(h, wg_ref[0], preferred_element_type=F32)
        u = jnp.dot(h, wu_ref[0], preferred_element_type=F32)
        a = (g * (1.0 / (1.0 + jnp.exp(-g))) * u).astype(BF16)
        acc_ref[...] += jnp.dot(a, wd_ref[0], preferred_element_type=F32)

    @pl.when(jnp.logical_and(active, j == last))
    def _():
        o_ref[...] = (acc_ref[...] * rw_ref[...]).astype(o_ref.dtype)

    @pl.when(jnp.logical_and(jnp.logical_not(active), j == last))
    def _():
        o_ref[...] = jnp.zeros_like(o_ref)


def _ffn(h, w_gate, w_up, w_down, tile_expert, n_active, row_weight, tm):
    P, D = h.shape
    F = w_gate.shape[2]
    tf = 512
    nf = F // tf

    def fidx(i, j, na):
        return jnp.where(i < na[0], j, nf - 1)

    grid_spec = pltpu.PrefetchScalarGridSpec(
        num_scalar_prefetch=2, grid=(P // tm, nf),
        in_specs=[
            pl.BlockSpec((tm, D), lambda i, j, te, na: (i, 0)),
            pl.BlockSpec((1, D, tf), lambda i, j, te, na: (te[i], 0, fidx(i, j, na))),
            pl.BlockSpec((1, D, tf), lambda i, j, te, na: (te[i], 0, fidx(i, j, na))),
            pl.BlockSpec((1, tf, D), lambda i, j, te, na: (te[i], fidx(i, j, na), 0)),
            pl.BlockSpec((tm, 1), lambda i, j, te, na: (i, 0)),
        ],
        out_specs=pl.BlockSpec((tm, D), lambda i, j, te, na: (i, 0)),
        scratch_shapes=[pltpu.VMEM((tm, D), F32)])
    return pl.pallas_call(
        _ffn_kernel, grid_spec=grid_spec,
        out_shape=jax.ShapeDtypeStruct((P, D), BF16),
        compiler_params=_params("parallel", "arbitrary"), name="swiglu_ffn")(
            tile_expert, n_active, h, w_gate, w_up, w_down, row_weight)


def _dense_ffn(h, w_gate, w_up, w_down):
    T = h.shape[0]
    tm = min(1024, T)
    n = T // tm
    return _ffn(h, w_gate[None].astype(BF16), w_up[None].astype(BF16), w_down[None].astype(BF16),
                jnp.zeros((n,), jnp.int32), jnp.full((1,), n, jnp.int32),
                jnp.ones((T, 1), F32), tm)


def _moe_ffn(h, logits, w_gate, w_up, w_down):
    T, D = h.shape
    E = N_EXPERTS
    tm = min(1024, T)
    probs = jax.nn.softmax(logits[:, :E], axis=-1)
    top_p, top_i = lax.top_k(probs, TOP_K)
    top_p = top_p / jnp.sum(top_p, axis=-1, keepdims=True)
    flat_e = top_i.reshape(-1).astype(jnp.int32)
    n_slots = T * TOP_K
    order = jnp.argsort(flat_e, stable=True).astype(jnp.int32)
    counts = jnp.sum(flat_e[:, None] == jnp.arange(E, dtype=jnp.int32)[None, :], axis=0).astype(jnp.int32)
    tiles_per = (counts + tm - 1) // tm
    tile_end = jnp.cumsum(tiles_per)
    row_start = (tile_end - tiles_per) * tm
    slot_start = jnp.cumsum(counts) - counts
    sorted_e = flat_e[order]
    dest_sorted = row_start[sorted_e] + jnp.arange(n_slots, dtype=jnp.int32) - slot_start[sorted_e]
    n_tiles = n_slots // tm + E
    P = n_tiles * tm
    src_token = jnp.zeros((P,), jnp.int32).at[dest_sorted].set(order // TOP_K)
    row_weight = jnp.zeros((P,), F32).at[dest_sorted].set(top_p.reshape(-1)[order])
    dest = jnp.zeros((n_slots,), jnp.int32).at[order].set(dest_sorted)
    n_active = tile_end[-1:].astype(jnp.int32)
    tile_ids = jnp.arange(n_tiles, dtype=jnp.int32)
    tile_expert = jnp.sum(tile_ids[:, None] >= tile_end[None, :], axis=1).astype(jnp.int32)
    tile_expert = jnp.minimum(tile_expert, tile_expert[jnp.maximum(n_active[0] - 1, 0)])
    hs = jnp.take(h, src_token, axis=0)
    y = _ffn(hs, w_gate.astype(BF16), w_up.astype(BF16), w_down.astype(BF16),
             tile_expert, n_active, row_weight.reshape(P, 1), tm)
    dest = dest.reshape(T, TOP_K)
    return jnp.take(y, dest[:, 0], axis=0), jnp.take(y, dest[:, 1], axis=0)


def kernel(x, attn_norm_g, w_in, na_q_g, na_k_g, na_rpb, mla_q_norm_g, mla_kv_norm_g, mla_w_uq, mla_w_ukv, mla_q_g, mla_k_g, hy_short_w, hy_short_b, hy_w1, hy_b1, hy_w2, hy_b2, hy_w3, hy_freq, hy_bias, group_norm_g, w_out, ffn_norm_g, dense_w_gate, dense_w_up, dense_w_down, router_w, moe_w_gate, moe_w_up, moe_w_down):
    B, L, D = x.shape
    T = B * L
    depth = attn_norm_g.shape[0]
    assert L % (NA_K_ROWS * GRID_W) == 0 and w_in.shape[2] == IN_WIDTH
    x2 = x.reshape(T, D).astype(F32)
    deltas = []
    for l in range(depth):
        x2, h = _add_norm(x2, deltas, attn_norm_g[l].astype(F32))
        w_in_p = jnp.pad(w_in[l], ((0, 0), (0, IN_WIDTH_PAD - IN_WIDTH))).astype(BF16)
        proj = _matmul(h, w_in_p, BF16, 1024)
        o_na = _neighborhood_attention(proj.reshape(B, L, IN_WIDTH_PAD), na_rpb[l],
                                       na_q_g[l], na_k_g[l]).reshape(T, NA_WIDTH)
        qn, qr, kn, kr, v = _mla_prep(proj, L, mla_q_norm_g[l], mla_kv_norm_g[l], mla_w_uq[l],
                                      mla_w_ukv[l], mla_q_g[l], mla_k_g[l])
        o_mla = _flash_attention(qn, qr, kn, kr, v, B, L).reshape(T, MLA_HEADS * MLA_V_DIM)
        hy_u = proj[:, COL_HY:IN_WIDTH].astype(F32).reshape(B, L, (HY_ORDER + 1) * HY_CH)
        o_hy = _hyena(hy_u, hy_short_w[l], hy_short_b[l], hy_w1[l], hy_b1[l], hy_w2[l], hy_b2[l],
                      hy_w3[l], hy_freq[l], hy_bias[l]).reshape(T, HY_CH)
        mix = _mix_norm(o_na, o_mla, o_hy, group_norm_g[l])
        d_mix = _matmul(mix, w_out[l].astype(BF16), BF16, 1024)
        i = l // 2
        if l % 2 == 0:
            x2, h = _add_norm(x2, [d_mix], ffn_norm_g[l].astype(F32))
            deltas = [_dense_ffn(h, dense_w_gate[i], dense_w_up[i], dense_w_down[i])]
        else:
            wr = jnp.pad(router_w[i].astype(F32), ((0, 0), (0, LANES - N_EXPERTS)))
            x2, h, logits = _add_norm(x2, [d_mix], ffn_norm_g[l].astype(F32), wr)
            deltas = list(_moe_ffn(h, logits, moe_w_gate[i], moe_w_up[i], moe_w_down[i]))
    (x2,) = _add_norm(x2, deltas)
    return x2.reshape(B, L, D).astype(x.dtype)
```

```python
import functools
import math

import jax
import jax.numpy as jnp
import numpy as np
from jax import lax
from jax.experimental import pallas as pl
from jax.experimental.pallas import tpu as pltpu

F32 = jnp.float32
BF16 = jnp.bfloat16

GRID_W = 64
NA_HEADS = 8
NA_HEAD_DIM = 64
NA_WIDTH = NA_HEADS * NA_HEAD_DIM
NA_WIN_ROWS = 8
NA_WIN_COLS = 16
MLA_HEADS = 8
MLA_NOPE_DIM = 128
MLA_ROPE_DIM = 64
MLA_V_DIM = 128
MLA_QK_DIM = MLA_NOPE_DIM + MLA_ROPE_DIM
MLA_Q_RANK = 512
MLA_KV_RANK = 256
ROPE_THETA = 10000.0
HY_CH = 512
HY_ORDER = 2
HY_POS_BANDS = 16
HY_DECAY_TARGET = 1e-2
HY_DECAY_FAST = 0.3
HY_DECAY_SLOW = 1.5
HY_MAX_DECAY = math.log(HY_DECAY_TARGET) / HY_DECAY_FAST
HY_MIN_DECAY = math.log(HY_DECAY_TARGET) / HY_DECAY_SLOW
N_EXPERTS = 8
TOP_K = 2
EPS = 1e-6
NEG_INF = -1e30

SRC_KROPE = 3 * NA_WIDTH + MLA_Q_RANK + MLA_KV_RANK
SRC_HY = SRC_KROPE + MLA_ROPE_DIM
IN_WIDTH = SRC_HY + (HY_ORDER + 1) * HY_CH
SRC_CKV = 3 * NA_WIDTH + MLA_Q_RANK
COL_NA_Q = 0
COL_NA_K = NA_WIDTH
COL_NA_V = 2 * NA_WIDTH
COL_CQ = 3 * NA_WIDTH
COL_HY = COL_CQ + MLA_Q_RANK
COL_CKV = COL_HY + (HY_ORDER + 1) * HY_CH
COL_KROPE = COL_CKV + MLA_KV_RANK
LANES = 128
IN_WIDTH_PAD = 4096

VMEM_LIMIT_BYTES = 56 * 1024 * 1024

NA_Q_ROWS = 8
NA_K_ROWS = 16


def _params(*sem):
    return pltpu.CompilerParams(dimension_semantics=sem, vmem_limit_bytes=VMEM_LIMIT_BYTES)


def _add_norm_kernel(*refs, n_delta, with_router, with_norm):
    x_ref = refs[0]
    d_refs = refs[1:1 + n_delta]
    pos = 1 + n_delta
    x = x_ref[...]
    for d in d_refs:
        x = x + d[...].astype(F32)
    if not with_norm:
        refs[pos][...] = x
        return
    g_ref = refs[pos]
    pos += 1
    if with_router:
        wr_ref = refs[pos]
        pos += 1
    xo_ref, h_ref = refs[pos], refs[pos + 1]
    xo_ref[...] = x
    h = x * lax.rsqrt(jnp.mean(x * x, axis=-1, keepdims=True) + EPS) * g_ref[...]
    h_ref[...] = h.astype(BF16)
    if with_router:
        refs[pos + 2][...] = jnp.dot(h, wr_ref[...], preferred_element_type=F32,
                                     precision=lax.Precision.HIGHEST)


def _add_norm(x, deltas, gain=None, router_w=None):
    T, D = x.shape
    tm = min(512, T)
    with_norm = gain is not None
    with_router = router_w is not None
    row = pl.BlockSpec((tm, D), lambda i: (i, 0))
    in_specs = [row] + [row] * len(deltas)
    args = [x] + list(deltas)
    out_shape = [jax.ShapeDtypeStruct((T, D), F32)]
    out_specs = [row]
    if with_norm:
        in_specs.append(pl.BlockSpec((1, D), lambda i: (0, 0)))
        args.append(gain.reshape(1, D))
        if with_router:
            in_specs.append(pl.BlockSpec((D, LANES), lambda i: (0, 0)))
            args.append(router_w)
        out_shape.append(jax.ShapeDtypeStruct((T, D), BF16))
        out_specs.append(row)
        if with_router:
            out_shape.append(jax.ShapeDtypeStruct((T, LANES), F32))
            out_specs.append(pl.BlockSpec((tm, LANES), lambda i: (i, 0)))
    out = pl.pallas_call(
        functools.partial(_add_norm_kernel, n_delta=len(deltas), with_router=with_router,
                          with_norm=with_norm),
        grid=(T // tm,), in_specs=in_specs, out_specs=out_specs, out_shape=out_shape,
        compiler_params=_params("parallel"), name="add_norm")(*args)
    return out


def _mm_kernel(a_ref, b_ref, o_ref):
    o_ref[...] = jnp.dot(a_ref[...], b_ref[...], preferred_element_type=F32).astype(o_ref.dtype)


def _matmul(a, b, out_dtype, tn):
    M, K = a.shape
    N = b.shape[1]
    tm = min(1024, M)
    return pl.pallas_call(
        _mm_kernel, grid=(M // tm, N // tn),
        in_specs=[pl.BlockSpec((tm, K), lambda i, j: (i, 0)),
                  pl.BlockSpec((K, tn), lambda i, j: (0, j))],
        out_specs=pl.BlockSpec((tm, tn), lambda i, j: (i, j)),
        out_shape=jax.ShapeDtypeStruct((M, N), out_dtype),
        compiler_params=_params("parallel", "parallel"), name="matmul")(a, b)


def _na_bias_table(rpb, rows):
    n_groups = rows // NA_Q_ROWS
    reps = (0, min(1, n_groups - 1), n_groups - 1)
    row_off = np.zeros((3, NA_Q_ROWS, NA_K_ROWS), np.int32)
    row_ok = np.zeros((3, NA_Q_ROWS, NA_K_ROWS), bool)
    for p, g in enumerate(reps):
        start = int(np.clip(g * NA_Q_ROWS - NA_WIN_ROWS // 2, 0, rows - NA_K_ROWS))
        for i in range(NA_Q_ROWS):
            r = g * NA_Q_ROWS + i
            rs = int(np.clip(r - NA_WIN_ROWS // 2, 0, rows - NA_WIN_ROWS))
            for j in range(NA_K_ROWS):
                kr = start + j
                ok = rs <= kr < rs + NA_WIN_ROWS
                row_ok[p, i, j] = ok
                row_off[p, i, j] = np.clip(kr - r + NA_WIN_ROWS - 1, 0, 2 * NA_WIN_ROWS - 2)
    cols = np.arange(GRID_W)
    col_start = np.clip(cols - NA_WIN_COLS // 2, 0, GRID_W - NA_WIN_COLS)
    col_ok = (cols[None, :] >= col_start[:, None]) & (cols[None, :] < col_start[:, None] + NA_WIN_COLS)
    col_off = np.clip(cols[None, :] - cols[:, None] + NA_WIN_COLS - 1, 0, 2 * NA_WIN_COLS - 2)
    row_sel = np.eye(2 * NA_WIN_ROWS - 1, dtype=np.float32)[row_off]
    col_sel = np.eye(2 * NA_WIN_COLS - 1, dtype=np.float32)[col_off]
    tab = jnp.einsum('hrc,pijr,abc->hpiajb', rpb.astype(F32), row_sel, col_sel,
                     precision=lax.Precision.HIGHEST)
    ok = row_ok[:, :, None, :, None] & col_ok[None, None, :, None, :]
    tab = jnp.where(ok[None], tab, NEG_INF)
    return tab.reshape(NA_HEADS, 3, NA_Q_ROWS * GRID_W, NA_K_ROWS * GRID_W)


def _na_kernel(q_ref, k_ref, v_ref, bias_ref, gq_ref, gk_ref, ones_ref, o_ref, *, rows):
    g = pl.program_id(2)
    tq = NA_Q_ROWS * GRID_W
    tk = NA_K_ROWS * GRID_W
    start = jnp.clip(g * NA_Q_ROWS - NA_WIN_ROWS // 2, 0, rows - NA_K_ROWS) * GRID_W
    start = pl.multiple_of(start, GRID_W)
    ones_bd = ones_ref[...]

    def head_norm(x, gain):
        x2 = x * x
        hi = x2.astype(BF16)
        lo = (x2 - hi.astype(F32)).astype(BF16)
        ssq = (jnp.dot(hi, ones_bd, preferred_element_type=F32)
               + jnp.dot(lo, ones_bd, preferred_element_type=F32))
        return x * lax.rsqrt(ssq * (1.0 / NA_HEAD_DIM) + EPS) * gain

    q = head_norm(q_ref[0].astype(F32), gq_ref[...]) * (NA_HEAD_DIM ** -0.5)
    kw = head_norm(k_ref[0, pl.ds(start, tk), :].astype(F32), gk_ref[...]).astype(BF16)
    vw = v_ref[0, pl.ds(start, tk), :]
    lane = lax.broadcasted_iota(jnp.int32, (tq, LANES), 1)
    outs = []
    for hh in range(2):
        sel = (lane < NA_HEAD_DIM) if hh == 0 else (lane >= NA_HEAD_DIM)
        qm = jnp.where(sel, q, 0.0).astype(BF16)
        s = lax.dot_general(qm, kw, (((1,), (1,)), ((), ())), preferred_element_type=F32)
        s = s + bias_ref[hh, 0]
        m = jnp.max(s, axis=-1, keepdims=True)
        p = jnp.exp(s - m)
        l = jnp.sum(p, axis=-1, keepdims=True)
        o = jnp.dot(p.astype(BF16), vw, preferred_element_type=F32)
        outs.append(o / l)
    o_ref[0] = jnp.where(lane < NA_HEAD_DIM, outs[0], outs[1]).astype(o_ref.dtype)


def _neighborhood_attention(proj3, rpb, q_g, k_g):
    B, L, _ = proj3.shape
    rows = L // GRID_W
    n_groups = rows // NA_Q_ROWS
    tq = NA_Q_ROWS * GRID_W
    tk = NA_K_ROWS * GRID_W
    bias = _na_bias_table(rpb, rows)
    gq = jnp.tile(q_g.astype(F32), 2).reshape(1, LANES)
    gk = jnp.tile(k_g.astype(F32), 2).reshape(1, LANES)
    ones_bd = jnp.asarray(np.kron(np.eye(2), np.ones((NA_HEAD_DIM, NA_HEAD_DIM))), BF16)
    qb, kb, vb = COL_NA_Q // LANES, COL_NA_K // LANES, COL_NA_V // LANES

    def pattern(g):
        return (g > 0).astype(jnp.int32) + (g == n_groups - 1).astype(jnp.int32)

    return pl.pallas_call(
        functools.partial(_na_kernel, rows=rows),
        grid=(NA_HEADS // 2, B, n_groups),
        in_specs=[
            pl.BlockSpec((1, tq, LANES), lambda hp, b, g: (b, g, qb + hp)),
            pl.BlockSpec((1, L, LANES), lambda hp, b, g: (b, 0, kb + hp)),
            pl.BlockSpec((1, L, LANES), lambda hp, b, g: (b, 0, vb + hp)),
            pl.BlockSpec((2, 1, tq, tk), lambda hp, b, g: (hp, pattern(g), 0, 0)),
            pl.BlockSpec((1, LANES), lambda hp, b, g: (0, 0)),
            pl.BlockSpec((1, LANES), lambda hp, b, g: (0, 0)),
            pl.BlockSpec((LANES, LANES), lambda hp, b, g: (0, 0)),
        ],
        out_specs=pl.BlockSpec((1, tq, LANES), lambda hp, b, g: (b, g, hp)),
        out_shape=jax.ShapeDtypeStruct((B, L, NA_WIDTH), BF16),
        compiler_params=_params("parallel", "parallel", "arbitrary"),
        name="neighborhood_attention")(proj3, proj3, proj3, bias, gq, gk, ones_bd)


def _mla_prep_kernel(cq_ref, ckv_ref, krope_ref, gqn_ref, gkvn_ref, wuq_ref, wukv_ref,
                     qgn_ref, qgr_ref, kgn_ref, kgr_ref, cos_ref, sin_ref,
                     qn_ref, qr_ref, kn_ref, kr_ref, v_ref):
    def rms(x, g):
        return x * lax.rsqrt(jnp.mean(x * x, axis=-1, keepdims=True) + EPS) * g

    tm = cq_ref.shape[0]
    nope_w = MLA_HEADS * MLA_NOPE_DIM
    cq = rms(cq_ref[...].astype(F32), gqn_ref[...]).astype(BF16)
    q = jnp.dot(cq, wuq_ref[...], preferred_element_type=F32)
    ckv = rms(ckv_ref[...].astype(F32), gkvn_ref[...]).astype(BF16)
    kv = jnp.dot(ckv, wukv_ref[...], preferred_element_type=F32)
    v_ref[...] = kv[:, nope_w:].astype(BF16)

    lane = lax.broadcasted_iota(jnp.int32, (tm, LANES), 1)
    first = lane < MLA_ROPE_DIM
    low_half = (lane % MLA_ROPE_DIM) < (MLA_ROPE_DIM // 2)
    cosv = cos_ref[...]
    sinv = sin_ref[...]
    scale = MLA_QK_DIM ** -0.5
    inv_dim = 1.0 / MLA_QK_DIM

    def rope(x):
        swapped = jnp.where(low_half, pltpu.roll(x, LANES - MLA_ROPE_DIM // 2, 1),
                            pltpu.roll(x, MLA_ROPE_DIM // 2, 1))
        return x * cosv + swapped * sinv

    qgn, qgr = qgn_ref[...], qgr_ref[...]
    kgn, kgr = kgn_ref[...], kgr_ref[...]
    kraw = krope_ref[...].astype(F32)
    kdup = jnp.where(first, kraw, pltpu.roll(kraw, MLA_ROPE_DIM, 1))
    k_rope_sq = jnp.where(first, kraw * kraw, 0.0)
    k_roped = rope(kdup * kgr)

    for j in range(MLA_HEADS // 2):
        qr_j = q[:, nope_w + LANES * j:nope_w + LANES * (j + 1)]
        qr_sq = qr_j * qr_j
        q_rs, k_rs = [], []
        for hh in range(2):
            h = 2 * j + hh
            cols = slice(LANES * h, LANES * (h + 1))
            qn_h = q[:, cols]
            t = qn_h * qn_h + jnp.where(first if hh == 0 else jnp.logical_not(first), qr_sq, 0.0)
            rs = lax.rsqrt(jnp.sum(t, axis=-1, keepdims=True) * inv_dim + EPS)
            qn_ref[:, cols] = (qn_h * rs * qgn * scale).astype(BF16)
            q_rs.append(rs)
            kn_h = kv[:, cols]
            t = kn_h * kn_h + k_rope_sq
            rs = lax.rsqrt(jnp.sum(t, axis=-1, keepdims=True) * inv_dim + EPS)
            kn_ref[:, cols] = (kn_h * rs * kgn).astype(BF16)
            k_rs.append(rs)
        pair = slice(LANES * j, LANES * (j + 1))
        qr_ref[:, pair] = (rope(qr_j * jnp.where(first, q_rs[0], q_rs[1]) * qgr) * scale).astype(BF16)
        kr_ref[:, pair] = (k_roped * jnp.where(first, k_rs[0], k_rs[1])).astype(BF16)


def _mla_prep(proj, L, q_norm_g, kv_norm_g, w_uq, w_ukv, q_head_g, k_head_g):
    T = proj.shape[0]
    tm = min(512, L)
    H = MLA_HEADS
    wq = w_uq.reshape(MLA_Q_RANK, H, MLA_QK_DIM)
    wq = jnp.concatenate([wq[:, :, :MLA_NOPE_DIM].reshape(MLA_Q_RANK, -1),
                          wq[:, :, MLA_NOPE_DIM:].reshape(MLA_Q_RANK, -1)], axis=1).astype(BF16)
    wkv = w_ukv.reshape(MLA_KV_RANK, H, MLA_NOPE_DIM + MLA_V_DIM)
    wkv = jnp.concatenate([wkv[:, :, :MLA_NOPE_DIM].reshape(MLA_KV_RANK, -1),
                           wkv[:, :, MLA_NOPE_DIM:].reshape(MLA_KV_RANK, -1)], axis=1).astype(BF16)
    half = MLA_ROPE_DIM // 2
    freqs = ROPE_THETA ** (-jnp.arange(half, dtype=F32) / half)
    ang = jnp.arange(L).astype(F32)[:, None] * freqs[None, :]
    cos, sin = jnp.cos(ang), jnp.sin(ang)
    cos2 = jnp.tile(jnp.concatenate([cos, cos], axis=1), (1, 2))
    sin2 = jnp.tile(jnp.concatenate([-sin, sin], axis=1), (1, 2))

    def vec(g, reps=1):
        return jnp.tile(g.astype(F32), reps).reshape(1, -1)

    const = lambda shape: pl.BlockSpec(shape, lambda i: (0, 0))
    nblk = L // tm
    outs = pl.pallas_call(
        _mla_prep_kernel, grid=(T // tm,),
        in_specs=[
            pl.BlockSpec((tm, MLA_Q_RANK), lambda i: (i, COL_CQ // MLA_Q_RANK)),
            pl.BlockSpec((tm, MLA_KV_RANK), lambda i: (i, COL_CKV // MLA_KV_RANK)),
            pl.BlockSpec((tm, LANES), lambda i: (i, COL_KROPE // LANES)),
            const((1, MLA_Q_RANK)), const((1, MLA_KV_RANK)),
            const(wq.shape), const(wkv.shape),
            const((1, LANES)), const((1, LANES)), const((1, LANES)), const((1, LANES)),
            pl.BlockSpec((tm, LANES), lambda i: (i % nblk, 0)),
            pl.BlockSpec((tm, LANES), lambda i: (i % nblk, 0)),
        ],
        out_specs=[
            pl.BlockSpec((tm, H * MLA_NOPE_DIM), lambda i: (i, 0)),
            pl.BlockSpec((tm, H * MLA_ROPE_DIM), lambda i: (i, 0)),
            pl.BlockSpec((tm, H * MLA_NOPE_DIM), lambda i: (i, 0)),
            pl.BlockSpec((tm, H * MLA_ROPE_DIM), lambda i: (i, 0)),
            pl.BlockSpec((tm, H * MLA_V_DIM), lambda i: (i, 0)),
        ],
        out_shape=[
            jax.ShapeDtypeStruct((T, H * MLA_NOPE_DIM), BF16),
            jax.ShapeDtypeStruct((T, H * MLA_ROPE_DIM), BF16),
            jax.ShapeDtypeStruct((T, H * MLA_NOPE_DIM), BF16),
            jax.ShapeDtypeStruct((T, H * MLA_ROPE_DIM), BF16),
            jax.ShapeDtypeStruct((T, H * MLA_V_DIM), BF16),
        ],
        compiler_params=_params("parallel"), name="mla_prep")(
            proj, proj, proj, vec(q_norm_g), vec(kv_norm_g), wq, wkv,
            vec(q_head_g[:MLA_NOPE_DIM]), vec(q_head_g[MLA_NOPE_DIM:], 2),
            vec(k_head_g[:MLA_NOPE_DIM]), vec(k_head_g[MLA_NOPE_DIM:], 2), cos2, sin2)
    return outs


def _flash_kernel(qn_ref, qr_ref, kn_ref, kr_ref, v_ref, o_ref, q_sc, m_sc, l_sc, acc_sc):
    h = pl.program_id(1)
    ki = pl.program_id(3)
    tq = qn_ref.shape[1]

    @pl.when(ki == 0)
    def _():
        lane = lax.broadcasted_iota(jnp.int32, (tq, LANES), 1)
        own = (lane // MLA_ROPE_DIM) == (h % 2)
        q_sc[:, :LANES] = qn_ref[0]
        q_sc[:, LANES:] = jnp.where(own, qr_ref[0], jnp.zeros_like(qr_ref[0]))
        m_sc[...] = jnp.full_like(m_sc, NEG_INF)
        l_sc[...] = jnp.zeros_like(l_sc)
        acc_sc[...] = jnp.zeros_like(acc_sc)

    k = jnp.concatenate([kn_ref[0], kr_ref[0]], axis=-1)
    s = lax.dot_general(q_sc[...], k, (((1,), (1,)), ((), ())), preferred_element_type=F32)
    m_prev = m_sc[...]
    m_new = jnp.maximum(m_prev, jnp.max(s, axis=-1, keepdims=True))
    alpha = jnp.exp(m_prev - m_new)
    p = jnp.exp(s - m_new)
    l_sc[...] = alpha * l_sc[...] + jnp.sum(p, axis=-1, keepdims=True)
    acc_sc[...] = alpha * acc_sc[...] + jnp.dot(p.astype(BF16), v_ref[0], preferred_element_type=F32)
    m_sc[...] = m_new

    @pl.when(ki == pl.num_programs(3) - 1)
    def _():
        o_ref[0] = (acc_sc[...] / l_sc[...]).astype(o_ref.dtype)


def _flash_attention(qn, qr, kn, kr, v, B, L):
    H = MLA_HEADS
    tq = min(1024, L)
    tk = min(1024, L)
    r3 = lambda a: a.reshape(B, L, a.shape[-1])
    return pl.pallas_call(
        _flash_kernel, grid=(B, H, L // tq, L // tk),
        in_specs=[
            pl.BlockSpec((1, tq, LANES), lambda b, h, qi, ki: (b, qi, h)),
            pl.BlockSpec((1, tq, LANES), lambda b, h, qi, ki: (b, qi, h // 2)),
            pl.BlockSpec((1, tk, LANES), lambda b, h, qi, ki: (b, ki, h)),
            pl.BlockSpec((1, tk, LANES), lambda b, h, qi, ki: (b, ki, h // 2)),
            pl.BlockSpec((1, tk, LANES), lambda b, h, qi, ki: (b, ki, h)),
        ],
        out_specs=pl.BlockSpec((1, tq, LANES), lambda b, h, qi, ki: (b, qi, h)),
        out_shape=jax.ShapeDtypeStruct((B, L, H * MLA_V_DIM), BF16),
        scratch_shapes=[pltpu.VMEM((tq, 2 * LANES), BF16), pltpu.VMEM((tq, 1), F32),
                        pltpu.VMEM((tq, 1), F32), pltpu.VMEM((tq, LANES), F32)],
        compiler_params=_params("parallel", "parallel", "parallel", "arbitrary"),
        name="mla_flash")(r3(qn), r3(qr), r3(kn), r3(kr), r3(v))


def _hyena_filters(L, w1, b1, w2, b2, w3, freq):
    t = jnp.linspace(0.0, 1.0, L, dtype=F32)[:, None]
    bands = jnp.linspace(1e-4, HY_POS_BANDS - 1, HY_POS_BANDS, dtype=F32)[None, :]
    w = 2.0 * math.pi * jnp.arange(L, dtype=F32)[:, None] / L
    z = jnp.concatenate([t, jnp.cos(bands * w), -jnp.sin(bands * w)], axis=-1)
    hp = lax.Precision.HIGHEST
    a = jnp.sin(freq[0] * (jnp.matmul(z, w1, precision=hp) + b1))
    a = jnp.sin(freq[1] * (jnp.matmul(a, w2, precision=hp) + b2))
    f = jnp.matmul(a, w3, precision=hp).reshape(L, 2, HY_ORDER, HY_CH)
    deltas = jnp.abs(jnp.linspace(HY_MIN_DECAY, HY_MAX_DECAY, HY_CH, dtype=F32))
    f = f * jnp.exp(-t * deltas[None, :])[:, None, None, :]
    k = jnp.concatenate([f[:, 0], jnp.zeros((1, HY_ORDER, HY_CH), F32), f[1:, 1][::-1]], axis=0)
    return k * lax.rsqrt(jnp.sum(k * k, axis=0, keepdims=True) + EPS)


HALO_ROWS = 16


def _short_conv_kernel(x_ref, xp_ref, xn_ref, w_ref, b_ref, o_ref, *, tiles_per_seq):
    i = pl.program_id(0)
    x = x_ref[...].astype(F32)
    tm = x.shape[0]
    row = lax.broadcasted_iota(jnp.int32, x.shape, 0)
    t = i % tiles_per_seq
    halo_prev = jnp.where(t == 0, 0.0, xp_ref[HALO_ROWS - 1:HALO_ROWS, :].astype(F32))
    halo_next = jnp.where(t == tiles_per_seq - 1, 0.0, xn_ref[0:1, :].astype(F32))
    prev = jnp.where(row == 0, halo_prev, pltpu.roll(x, 1, 0))
    nxt = jnp.where(row == tm - 1, halo_next, pltpu.roll(x, tm - 1, 0))
    o_ref[...] = prev * w_ref[0:1, :] + x * w_ref[1:2, :] + nxt * w_ref[2:3, :] + b_ref[...]


def _short_conv(proj, L, w, b):
    T = proj.shape[0]
    W = (HY_ORDER + 1) * HY_CH
    tm = min(1024, L)
    tc = 512
    c0 = COL_HY // tc
    hb = tm // HALO_ROWS
    n_halo = T // HALO_ROWS
    return pl.pallas_call(
        functools.partial(_short_conv_kernel, tiles_per_seq=L // tm),
        grid=(T // tm, W // tc),
        in_specs=[
            pl.BlockSpec((tm, tc), lambda i, c: (i, c0 + c)),
            pl.BlockSpec((HALO_ROWS, tc), lambda i, c: (jnp.maximum(i * hb - 1, 0), c0 + c)),
            pl.BlockSpec((HALO_ROWS, tc), lambda i, c: (jnp.minimum((i + 1) * hb, n_halo - 1), c0 + c)),
            pl.BlockSpec((3, tc), lambda i, c: (0, c)),
            pl.BlockSpec((1, tc), lambda i, c: (0, c)),
        ],
        out_specs=pl.BlockSpec((tm, tc), lambda i, c: (i, c)),
        out_shape=jax.ShapeDtypeStruct((T, W), F32),
        compiler_params=_params("parallel", "parallel"), name="hyena_short_conv")(
            proj, proj, proj, w.astype(F32), b.astype(F32).reshape(1, W))


HY_CB = LANES
HY_NLO = 128
HY_SLAB_CHUNK = 16


def _dft_tables(L):
    N = 2 * L
    nhi = N // HY_NLO
    kb = jnp.arange(nhi, dtype=jnp.int32)
    ang = (2.0 * math.pi / nhi) * ((kb[:, None] * kb[None, :]) % nhi).astype(F32)
    f1 = jnp.concatenate([jnp.cos(ang), -jnp.sin(ang)], axis=0)
    ka = jnp.arange(HY_NLO, dtype=jnp.int32)
    idx = (ka[None, None, :] * (ka[None, :, None] * nhi + kb[:, None, None])) % N
    ang = (2.0 * math.pi / N) * idx.astype(F32)
    gr, gi = jnp.cos(ang), -jnp.sin(ang)
    g = jnp.concatenate([jnp.concatenate([gr, -gi], axis=2),
                         jnp.concatenate([gi, gr], axis=2)], axis=1)
    return f1.astype(BF16), g.astype(BF16)


def _dft_major_stage(src_ref, f1, a_ref, n_rows):
    two_nhi = f1.shape[0]

    def body(n_lo, carry):
        xs = src_ref[pl.ds(n_lo, n_rows, stride=HY_NLO), :]
        dst = pl.multiple_of(n_lo * two_nhi, two_nhi)
        a_ref[pl.ds(dst, two_nhi), :] = jnp.dot(f1, xs.astype(BF16), preferred_element_type=F32)
        return carry

    lax.fori_loop(0, HY_NLO, body, 0)


def _load_slab(a_ref, kb, nhi):
    re = a_ref[pl.ds(kb, HY_NLO, stride=2 * nhi), :]
    im = a_ref[pl.ds(nhi + kb, HY_NLO, stride=2 * nhi), :]
    return jnp.concatenate([re, im], axis=0).astype(BF16)


def _hyena_spectrum_kernel(k_ref, f1_ref, g_ref, o_ref, a_ref, *, nhi):
    j = pl.program_id(1)

    @pl.when(j == 0)
    def _():
        _dft_major_stage(k_ref, f1_ref[...], a_ref, nhi)

    def slab(s, carry):
        kb = j * HY_SLAB_CHUNK + s
        o_ref[0, s] = jnp.dot(g_ref[s], _load_slab(a_ref, kb, nhi), preferred_element_type=F32)
        return carry

    lax.fori_loop(0, HY_SLAB_CHUNK, slab, 0)


def _hyena_spectrum(k2, f1, g):
    N, n_ch = k2.shape
    nhi = N // HY_NLO
    return pl.pallas_call(
        functools.partial(_hyena_spectrum_kernel, nhi=nhi),
        grid=(n_ch // HY_CB, nhi // HY_SLAB_CHUNK),
        in_specs=[
            pl.BlockSpec((N, HY_CB), lambda c, j: (0, c)),
            pl.BlockSpec(f1.shape, lambda c, j: (0, 0)),
            pl.BlockSpec((HY_SLAB_CHUNK, 2 * HY_NLO, 2 * HY_NLO), lambda c, j: (j, 0, 0)),
        ],
        out_specs=pl.BlockSpec((1, HY_SLAB_CHUNK, 2 * HY_NLO, HY_CB), lambda c, j: (c, j, 0, 0)),
        out_shape=jax.ShapeDtypeStruct((n_ch // HY_CB, nhi, 2 * HY_NLO, HY_CB), F32),
        scratch_shapes=[pltpu.VMEM((HY_NLO * 2 * nhi, HY_CB), F32)],
        compiler_params=_params("parallel", "arbitrary"), name="hyena_spectrum")(k2, f1, g)


def _hyena_conv_kernel(u_ref, gate_ref, bias_ref, f1_ref, f1t_ref, g_ref, kf_ref, o_ref, a_ref, *, nhi):
    j = pl.program_id(2)
    n_in = nhi // 2
    L = n_in * HY_NLO

    @pl.when(j == 0)
    def _():
        _dft_major_stage(u_ref.at[0], f1_ref[...], a_ref, n_in)

    def slab(s, carry):
        kb = j * HY_SLAB_CHUNK + s
        g = g_ref[s]
        t = jnp.dot(g, _load_slab(a_ref, kb, nhi), preferred_element_type=F32)
        kf = kf_ref[0, s]
        tr, ti = t[:HY_NLO], t[HY_NLO:]
        kr, ki = kf[:HY_NLO], kf[HY_NLO:]
        y = jnp.concatenate([tr * kr - ti * ki, tr * ki + ti * kr], axis=0).astype(BF16)
        r = lax.dot_general(g, y, (((0,), (0,)), ((), ())), preferred_element_type=F32)
        a_ref[pl.ds(kb, HY_NLO, stride=2 * nhi), :] = r[:HY_NLO]
        a_ref[pl.ds(nhi + kb, HY_NLO, stride=2 * nhi), :] = r[HY_NLO:]
        return carry

    lax.fori_loop(0, HY_SLAB_CHUNK, slab, 0)

    @pl.when(j == pl.num_programs(2) - 1)
    def _():
        f1t = f1t_ref[...]

        def body(n_lo, carry):
            src = pl.multiple_of(n_lo * 2 * nhi, 2 * nhi)
            blk = a_ref[pl.ds(src, 2 * nhi), :].astype(BF16)
            o_ref[0, pl.ds(n_lo, n_in, stride=HY_NLO), :] = jnp.dot(
                f1t, blk, preferred_element_type=F32)
            return carry

        lax.fori_loop(0, HY_NLO, body, 0)
        inv_n = 1.0 / (2 * L)
        rows = 512

        def gate_rows(c, carry):
            r0 = pl.multiple_of(c * rows, rows)
            sl = pl.ds(r0, rows)
            o_ref[0, sl, :] = gate_ref[0, sl, :] * (o_ref[0, sl, :] * inv_n
                                                    + u_ref[0, sl, :] * bias_ref[...])
            return carry

        lax.fori_loop(0, L // rows, gate_rows, 0)


def _hyena_conv(u3, u_col, gate3, gate_col, bias, kf, kf_row, f1, g):
    B, L, _ = u3.shape
    nhi = 2 * L // HY_NLO
    f1_in = f1[:, :nhi // 2]
    n_cb = HY_CH // HY_CB
    return pl.pallas_call(
        functools.partial(_hyena_conv_kernel, nhi=nhi),
        grid=(B, n_cb, nhi // HY_SLAB_CHUNK),
        in_specs=[
            pl.BlockSpec((1, L, HY_CB), lambda b, c, j: (b, 0, u_col + c)),
            pl.BlockSpec((1, L, HY_CB), lambda b, c, j: (b, 0, gate_col + c)),
            pl.BlockSpec((1, HY_CB), lambda b, c, j: (0, c)),
            pl.BlockSpec(f1_in.shape, lambda b, c, j: (0, 0)),
            pl.BlockSpec(f1_in.shape[::-1], lambda b, c, j: (0, 0)),
            pl.BlockSpec((HY_SLAB_CHUNK, 2 * HY_NLO, 2 * HY_NLO), lambda b, c, j: (j, 0, 0)),
            pl.BlockSpec((1, HY_SLAB_CHUNK, 2 * HY_NLO, HY_CB), lambda b, c, j: (kf_row + c, j, 0, 0)),
        ],
        out_specs=pl.BlockSpec((1, L, HY_CB), lambda b, c, j: (b, 0, c)),
        out_shape=jax.ShapeDtypeStruct((B, L, HY_CH), F32),
        scratch_shapes=[pltpu.VMEM((HY_NLO * 2 * nhi, HY_CB), F32)],
        compiler_params=_params("parallel", "parallel", "arbitrary"), name="hyena_long_conv")(
            u3, gate3, bias.astype(F32).reshape(1, HY_CH), f1_in, f1_in.T, g, kf)


def _hyena(proj, B, L, short_w, short_b, w1, b1, w2, b2, w3, freq, bias, f1, g):
    u = _short_conv(proj, L, short_w, short_b).reshape(B, L, (HY_ORDER + 1) * HY_CH)
    k = _hyena_filters(L, w1, b1, w2, b2, w3, freq).reshape(2 * L, HY_ORDER * HY_CH)
    kf = _hyena_spectrum(k, f1, g)
    n_cb = HY_CH // HY_CB
    z = _hyena_conv(u, 0, u, n_cb, bias[0], kf, 0, f1, g)
    return _hyena_conv(z, 0, u, 2 * n_cb, bias[1], kf, n_cb, f1, g)


def _mix_norm_kernel(na_ref, mla_ref, hy_ref, g_ref, o_ref):
    def rms(x, g):
        return (x * lax.rsqrt(jnp.mean(x * x, axis=-1, keepdims=True) + EPS) * g).astype(BF16)

    a, b = NA_WIDTH, NA_WIDTH + MLA_HEADS * MLA_V_DIM
    o_ref[:, :a] = rms(na_ref[...].astype(F32), g_ref[:, :a])
    o_ref[:, a:b] = rms(mla_ref[...].astype(F32), g_ref[:, a:b])
    o_ref[:, b:] = rms(hy_ref[...].astype(F32), g_ref[:, b:])


def _mix_norm(o_na, o_mla, o_hy, gain):
    T = o_na.shape[0]
    W = o_na.shape[1] + o_mla.shape[1] + o_hy.shape[1]
    tm = min(1024, T)
    row = lambda a: pl.BlockSpec((tm, a.shape[1]), lambda i: (i, 0))
    return pl.pallas_call(
        _mix_norm_kernel, grid=(T // tm,),
        in_specs=[row(o_na), row(o_mla), row(o_hy), pl.BlockSpec((1, W), lambda i: (0, 0))],
        out_specs=pl.BlockSpec((tm, W), lambda i: (i, 0)),
        out_shape=jax.ShapeDtypeStruct((T, W), BF16),
        compiler_params=_params("parallel"), name="mix_norm")(
            o_na, o_mla, o_hy, gain.astype(F32).reshape(1, W))


def _ffn_kernel(te_ref, na_ref, h_ref, wg_ref, wu_ref, wd_ref, rw_ref, o_ref, acc_ref):
    i = pl.program_id(0)
    j = pl.program_id(1)
    last = pl.num_programs(1) - 1
    active = i < na_ref[0]

    @pl.when(jnp.logical_and(active, j == 0))
    def _():
        acc_ref[...] = jnp.zeros_like(acc_ref)

    @pl.when(active)
    def _():
        h = h_ref[...]
        g = jnp.dot(h, wg_ref[0], preferred_element_type=F32)
        u = jnp.dot(h, wu_ref[0], preferred_element_type=F32)
        a = (g * (1.0 / (1.0 + jnp.exp(-g))) * u).astype(BF16)
        acc_ref[...] += jnp.dot(a, wd_ref[0], preferred_element_type=F32)

    @pl.when(jnp.logical_and(active, j == last))
    def _():
        o_ref[...] = (acc_ref[...] * rw_ref[...]).astype(o_ref.dtype)

    @pl.when(jnp.logical_and(jnp.logical_not(active), j == last))
    def _():
        o_ref[...] = jnp.zeros_like(o_ref)


def _ffn(h, w_gate, w_up, w_down, tile_expert, n_active, row_weight, tm):
    P, D = h.shape
    F = w_gate.shape[2]
    tf = 512
    nf = F // tf

    def fidx(i, j, na):
        return jnp.where(i < na[0], j, nf - 1)

    grid_spec = pltpu.PrefetchScalarGridSpec(
        num_scalar_prefetch=2, grid=(P // tm, nf),
        in_specs=[
            pl.BlockSpec((tm, D), lambda i, j, te, na: (i, 0)),
            pl.BlockSpec((1, D, tf), lambda i, j, te, na: (te[i], 0, fidx(i, j, na))),
            pl.BlockSpec((1, D, tf), lambda i, j, te, na: (te[i], 0, fidx(i, j, na))),
            pl.BlockSpec((1, tf, D), lambda i, j, te, na: (te[i], fidx(i, j, na), 0)),
            pl.BlockSpec((tm, 1), lambda i, j, te, na: (i, 0)),
        ],
        out_specs=pl.BlockSpec((tm, D), lambda i, j, te, na: (i, 0)),
        scratch_shapes=[pltpu.VMEM((tm, D), F32)])
    return pl.pallas_call(
        _ffn_kernel, grid_spec=grid_spec,
        out_shape=jax.ShapeDtypeStruct((P, D), BF16),
        compiler_params=_params("parallel", "arbitrary"), name="swiglu_ffn")(
            tile_expert, n_active, h, w_gate, w_up, w_down, row_weight)


def _dense_ffn(h, w_gate, w_up, w_down):
    T = h.shape[0]
    tm = min(1024, T)
    n = T // tm
    return _ffn(h, w_gate[None].astype(BF16), w_up[None].astype(BF16), w_down[None].astype(BF16),
                jnp.zeros((n,), jnp.int32), jnp.full((1,), n, jnp.int32),
                jnp.ones((T, 1), F32), tm)


def _moe_ffn(h, logits, w_gate, w_up, w_down):
    T, D = h.shape
    E = N_EXPERTS
    tm = min(1024, T)
    probs = jax.nn.softmax(logits[:, :E], axis=-1)
    top_p, top_i = lax.top_k(probs, TOP_K)
    top_p = top_p / jnp.sum(top_p, axis=-1, keepdims=True)
    flat_e = top_i.reshape(-1).astype(jnp.int32)
    n_slots = T * TOP_K
    order = jnp.argsort(flat_e, stable=True).astype(jnp.int32)
    counts = jnp.sum(flat_e[:, None] == jnp.arange(E, dtype=jnp.int32)[None, :], axis=0).astype(jnp.int32)
    tiles_per = (counts + tm - 1) // tm
    tile_end = jnp.cumsum(tiles_per)
    row_start = (tile_end - tiles_per) * tm
    slot_start = jnp.cumsum(counts) - counts
    sorted_e = flat_e[order]
    dest_sorted = row_start[sorted_e] + jnp.arange(n_slots, dtype=jnp.int32) - slot_start[sorted_e]
    n_tiles = n_slots // tm + E
    P = n_tiles * tm
    src_token = jnp.zeros((P,), jnp.int32).at[dest_sorted].set(order // TOP_K)
    row_weight = jnp.zeros((P,), F32).at[dest_sorted].set(top_p.reshape(-1)[order])
    dest = jnp.zeros((n_slots,), jnp.int32).at[order].set(dest_sorted)
    n_active = tile_end[-1:].astype(jnp.int32)
    tile_ids = jnp.arange(n_tiles, dtype=jnp.int32)
    tile_expert = jnp.sum(tile_ids[:, None] >= tile_end[None, :], axis=1).astype(jnp.int32)
    tile_expert = jnp.minimum(tile_expert, tile_expert[jnp.maximum(n_active[0] - 1, 0)])
    hs = jnp.take(h, src_token, axis=0)
    y = _ffn(hs, w_gate.astype(BF16), w_up.astype(BF16), w_down.astype(BF16),
             tile_expert, n_active, row_weight.reshape(P, 1), tm)
    dest = dest.reshape(T, TOP_K)
    return jnp.take(y, dest[:, 0], axis=0), jnp.take(y, dest[:, 1], axis=0)


def kernel(x, attn_norm_g, w_in, na_q_g, na_k_g, na_rpb, mla_q_norm_g, mla_kv_norm_g, mla_w_uq, mla_w_ukv, mla_q_g, mla_k_g, hy_short_w, hy_short_b, hy_w1, hy_b1, hy_w2, hy_b2, hy_w3, hy_freq, hy_bias, group_norm_g, w_out, ffn_norm_g, dense_w_gate, dense_w_up, dense_w_down, router_w, moe_w_gate, moe_w_up, moe_w_down):
    B, L, D = x.shape
    T = B * L
    depth = attn_norm_g.shape[0]
    assert L % (NA_K_ROWS * GRID_W) == 0 and w_in.shape[2] == IN_WIDTH
    x2 = x.reshape(T, D).astype(F32)
    f1, g_dft = _dft_tables(L)
    deltas = []
    for l in range(depth):
        x2, h = _add_norm(x2, deltas, attn_norm_g[l].astype(F32))
        w = w_in[l]
        w_in_p = jnp.concatenate(
            [w[:, :SRC_CKV], w[:, SRC_HY:], w[:, SRC_CKV:SRC_HY],
             jnp.zeros((D, IN_WIDTH_PAD - IN_WIDTH), w.dtype)], axis=1).astype(BF16)
        proj = _matmul(h, w_in_p, BF16, 1024)
        o_na = _neighborhood_attention(proj.reshape(B, L, IN_WIDTH_PAD), na_rpb[l],
                                       na_q_g[l], na_k_g[l]).reshape(T, NA_WIDTH)
        qn, qr, kn, kr, v = _mla_prep(proj, L, mla_q_norm_g[l], mla_kv_norm_g[l], mla_w_uq[l],
                                      mla_w_ukv[l], mla_q_g[l], mla_k_g[l])
        o_mla = _flash_attention(qn, qr, kn, kr, v, B, L).reshape(T, MLA_HEADS * MLA_V_DIM)
        o_hy = _hyena(proj, B, L, hy_short_w[l], hy_short_b[l], hy_w1[l], hy_b1[l], hy_w2[l],
                      hy_b2[l], hy_w3[l], hy_freq[l], hy_bias[l], f1, g_dft).reshape(T, HY_CH)
        mix = _mix_norm(o_na, o_mla, o_hy, group_norm_g[l])
        d_mix = _matmul(mix, w_out[l].astype(BF16), BF16, 1024)
        i = l // 2
        if l % 2 == 0:
            x2, h = _add_norm(x2, [d_mix], ffn_norm_g[l].astype(F32))
            deltas = [_dense_ffn(h, dense_w_gate[i], dense_w_up[i], dense_w_down[i])]
        else:
            wr = jnp.pad(router_w[i].astype(F32), ((0, 0), (0, LANES - N_EXPERTS)))
            x2, h, logits = _add_norm(x2, [d_mix], ffn_norm_g[l].astype(F32), wr)
            deltas = list(_moe_ffn(h, logits, moe_w_gate[i], moe_w_up[i], moe_w_down[i]))
    (x2,) = _add_norm(x2, deltas)
    return x2.reshape(B, L, D).astype(x.dtype)
```

```python
import functools
import math

import jax
import jax.numpy as jnp
import numpy as np
from jax import lax
from jax.experimental import pallas as pl
from jax.experimental.pallas import tpu as pltpu

F32 = jnp.float32
BF16 = jnp.bfloat16

GRID_W = 64
NA_HEADS = 8
NA_HEAD_DIM = 64
NA_WIDTH = NA_HEADS * NA_HEAD_DIM
NA_WIN_ROWS = 8
NA_WIN_COLS = 16
MLA_HEADS = 8
MLA_NOPE_DIM = 128
MLA_ROPE_DIM = 64
MLA_V_DIM = 128
MLA_QK_DIM = MLA_NOPE_DIM + MLA_ROPE_DIM
MLA_Q_RANK = 512
MLA_KV_RANK = 256
ROPE_THETA = 10000.0
HY_CH = 512
HY_ORDER = 2
HY_POS_BANDS = 16
HY_DECAY_TARGET = 1e-2
HY_DECAY_FAST = 0.3
HY_DECAY_SLOW = 1.5
HY_MAX_DECAY = math.log(HY_DECAY_TARGET) / HY_DECAY_FAST
HY_MIN_DECAY = math.log(HY_DECAY_TARGET) / HY_DECAY_SLOW
N_EXPERTS = 8
TOP_K = 2
EPS = 1e-6
NEG_INF = -1e30

SRC_KROPE = 3 * NA_WIDTH + MLA_Q_RANK + MLA_KV_RANK
SRC_HY = SRC_KROPE + MLA_ROPE_DIM
IN_WIDTH = SRC_HY + (HY_ORDER + 1) * HY_CH
SRC_CKV = 3 * NA_WIDTH + MLA_Q_RANK
COL_NA_Q = 0
COL_NA_K = NA_WIDTH
COL_NA_V = 2 * NA_WIDTH
COL_CQ = 3 * NA_WIDTH
COL_HY = COL_CQ + MLA_Q_RANK
COL_CKV = COL_HY + (HY_ORDER + 1) * HY_CH
COL_KROPE = COL_CKV + MLA_KV_RANK
LANES = 128
IN_WIDTH_PAD = 4096

VMEM_LIMIT_BYTES = 56 * 1024 * 1024

NA_Q_ROWS = 8
NA_K_ROWS = 16


def _params(*sem):
    return pltpu.CompilerParams(dimension_semantics=sem, vmem_limit_bytes=VMEM_LIMIT_BYTES)


def _add_norm_kernel(*refs, n_delta, with_router, with_norm):
    x_ref = refs[0]
    d_refs = refs[1:1 + n_delta]
    pos = 1 + n_delta
    x = x_ref[...]
    for d in d_refs:
        x = x + d[...].astype(F32)
    if not with_norm:
        refs[pos][...] = x
        return
    g_ref = refs[pos]
    pos += 1
    if with_router:
        wr_ref = refs[pos]
        pos += 1
    xo_ref, h_ref = refs[pos], refs[pos + 1]
    xo_ref[...] = x
    h = x * lax.rsqrt(jnp.mean(x * x, axis=-1, keepdims=True) + EPS) * g_ref[...]
    h_ref[...] = h.astype(BF16)
    if with_router:
        refs[pos + 2][...] = jnp.dot(h, wr_ref[...], preferred_element_type=F32,
                                     precision=lax.Precision.HIGHEST)


def _add_norm(x, deltas, gain=None, router_w=None):
    T, D = x.shape
    tm = min(512, T)
    with_norm = gain is not None
    with_router = router_w is not None
    row = pl.BlockSpec((tm, D), lambda i: (i, 0))
    in_specs = [row] + [row] * len(deltas)
    args = [x] + list(deltas)
    out_shape = [jax.ShapeDtypeStruct((T, D), F32)]
    out_specs = [row]
    if with_norm:
        in_specs.append(pl.BlockSpec((1, D), lambda i: (0, 0)))
        args.append(gain.reshape(1, D))
        if with_router:
            in_specs.append(pl.BlockSpec((D, LANES), lambda i: (0, 0)))
            args.append(router_w)
        out_shape.append(jax.ShapeDtypeStruct((T, D), BF16))
        out_specs.append(row)
        if with_router:
            out_shape.append(jax.ShapeDtypeStruct((T, LANES), F32))
            out_specs.append(pl.BlockSpec((tm, LANES), lambda i: (i, 0)))
    out = pl.pallas_call(
        functools.partial(_add_norm_kernel, n_delta=len(deltas), with_router=with_router,
                          with_norm=with_norm),
        grid=(T // tm,), in_specs=in_specs, out_specs=out_specs, out_shape=out_shape,
        compiler_params=_params("parallel"), name="add_norm")(*args)
    return out


def _mm_kernel(a_ref, b_ref, o_ref):
    o_ref[...] = jnp.dot(a_ref[...], b_ref[...], preferred_element_type=F32).astype(o_ref.dtype)


def _matmul(a, b, out_dtype, tn):
    M, K = a.shape
    N = b.shape[1]
    tm = min(1024, M)
    return pl.pallas_call(
        _mm_kernel, grid=(M // tm, N // tn),
        in_specs=[pl.BlockSpec((tm, K), lambda i, j: (i, 0)),
                  pl.BlockSpec((K, tn), lambda i, j: (0, j))],
        out_specs=pl.BlockSpec((tm, tn), lambda i, j: (i, j)),
        out_shape=jax.ShapeDtypeStruct((M, N), out_dtype),
        compiler_params=_params("parallel", "parallel"), name="matmul")(a, b)


def _na_bias_table(rpb, rows):
    n_groups = rows // NA_Q_ROWS
    reps = (0, min(1, n_groups - 1), n_groups - 1)
    row_off = np.zeros((3, NA_Q_ROWS, NA_K_ROWS), np.int32)
    row_ok = np.zeros((3, NA_Q_ROWS, NA_K_ROWS), bool)
    for p, g in enumerate(reps):
        start = int(np.clip(g * NA_Q_ROWS - NA_WIN_ROWS // 2, 0, rows - NA_K_ROWS))
        for i in range(NA_Q_ROWS):
            r = g * NA_Q_ROWS + i
            rs = int(np.clip(r - NA_WIN_ROWS // 2, 0, rows - NA_WIN_ROWS))
            for j in range(NA_K_ROWS):
                kr = start + j
                ok = rs <= kr < rs + NA_WIN_ROWS
                row_ok[p, i, j] = ok
                row_off[p, i, j] = np.clip(kr - r + NA_WIN_ROWS - 1, 0, 2 * NA_WIN_ROWS - 2)
    cols = np.arange(GRID_W)
    col_start = np.clip(cols - NA_WIN_COLS // 2, 0, GRID_W - NA_WIN_COLS)
    col_ok = (cols[None, :] >= col_start[:, None]) & (cols[None, :] < col_start[:, None] + NA_WIN_COLS)
    col_off = np.clip(cols[None, :] - cols[:, None] + NA_WIN_COLS - 1, 0, 2 * NA_WIN_COLS - 2)
    row_sel = np.eye(2 * NA_WIN_ROWS - 1, dtype=np.float32)[row_off]
    col_sel = np.eye(2 * NA_WIN_COLS - 1, dtype=np.float32)[col_off]
    tab = jnp.einsum('hrc,pijr,abc->hpiajb', rpb.astype(F32), row_sel, col_sel,
                     precision=lax.Precision.HIGHEST)
    ok = row_ok[:, :, None, :, None] & col_ok[None, None, :, None, :]
    tab = jnp.where(ok[None], tab, NEG_INF)
    return tab.reshape(NA_HEADS, 3, NA_Q_ROWS * GRID_W, NA_K_ROWS * GRID_W)


def _na_kernel(q_ref, k_ref, v_ref, bias_ref, gq_ref, gk_ref, ones_ref, o_ref, *, rows):
    g = pl.program_id(2)
    tq = NA_Q_ROWS * GRID_W
    tk = NA_K_ROWS * GRID_W
    start = jnp.clip(g * NA_Q_ROWS - NA_WIN_ROWS // 2, 0, rows - NA_K_ROWS) * GRID_W
    start = pl.multiple_of(start, GRID_W)
    ones_bd = ones_ref[...]

    def head_norm(x, gain):
        x2 = x * x
        hi = x2.astype(BF16)
        lo = (x2 - hi.astype(F32)).astype(BF16)
        ssq = (jnp.dot(hi, ones_bd, preferred_element_type=F32)
               + jnp.dot(lo, ones_bd, preferred_element_type=F32))
        return x * lax.rsqrt(ssq * (1.0 / NA_HEAD_DIM) + EPS) * gain

    q = head_norm(q_ref[0].astype(F32), gq_ref[...]) * (NA_HEAD_DIM ** -0.5)
    kw = head_norm(k_ref[0, pl.ds(start, tk), :].astype(F32), gk_ref[...]).astype(BF16)
    vw = v_ref[0, pl.ds(start, tk), :]
    lane = lax.broadcasted_iota(jnp.int32, (tq, LANES), 1)
    outs = []
    for hh in range(2):
        sel = (lane < NA_HEAD_DIM) if hh == 0 else (lane >= NA_HEAD_DIM)
        qm = jnp.where(sel, q, 0.0).astype(BF16)
        s = lax.dot_general(qm, kw, (((1,), (1,)), ((), ())), preferred_element_type=F32)
        s = s + bias_ref[hh, 0]
        m = jnp.max(s, axis=-1, keepdims=True)
        p = jnp.exp(s - m)
        l = jnp.sum(p, axis=-1, keepdims=True)
        o = jnp.dot(p.astype(BF16), vw, preferred_element_type=F32)
        outs.append(o / l)
    o_ref[0] = jnp.where(lane < NA_HEAD_DIM, outs[0], outs[1]).astype(o_ref.dtype)


def _neighborhood_attention(proj3, rpb, q_g, k_g):
    B, L, _ = proj3.shape
    rows = L // GRID_W
    n_groups = rows // NA_Q_ROWS
    tq = NA_Q_ROWS * GRID_W
    tk = NA_K_ROWS * GRID_W
    bias = _na_bias_table(rpb, rows)
    gq = jnp.tile(q_g.astype(F32), 2).reshape(1, LANES)
    gk = jnp.tile(k_g.astype(F32), 2).reshape(1, LANES)
    ones_bd = jnp.asarray(np.kron(np.eye(2), np.ones((NA_HEAD_DIM, NA_HEAD_DIM))), BF16)
    qb, kb, vb = COL_NA_Q // LANES, COL_NA_K // LANES, COL_NA_V // LANES

    def pattern(g):
        return (g > 0).astype(jnp.int32) + (g == n_groups - 1).astype(jnp.int32)

    return pl.pallas_call(
        functools.partial(_na_kernel, rows=rows),
        grid=(NA_HEADS // 2, B, n_groups),
        in_specs=[
            pl.BlockSpec((1, tq, LANES), lambda hp, b, g: (b, g, qb + hp)),
            pl.BlockSpec((1, L, LANES), lambda hp, b, g: (b, 0, kb + hp)),
            pl.BlockSpec((1, L, LANES), lambda hp, b, g: (b, 0, vb + hp)),
            pl.BlockSpec((2, 1, tq, tk), lambda hp, b, g: (hp, pattern(g), 0, 0)),
            pl.BlockSpec((1, LANES), lambda hp, b, g: (0, 0)),
            pl.BlockSpec((1, LANES), lambda hp, b, g: (0, 0)),
            pl.BlockSpec((LANES, LANES), lambda hp, b, g: (0, 0)),
        ],
        out_specs=pl.BlockSpec((1, tq, LANES), lambda hp, b, g: (b, g, hp)),
        out_shape=jax.ShapeDtypeStruct((B, L, NA_WIDTH), BF16),
        compiler_params=_params("parallel", "parallel", "arbitrary"),
        name="neighborhood_attention")(proj3, proj3, proj3, bias, gq, gk, ones_bd)


def _mla_prep_kernel(cq_ref, ckv_ref, krope_ref, gqn_ref, gkvn_ref, wuq_ref, wukv_ref,
                     qgn_ref, qgr_ref, kgn_ref, kgr_ref, cos_ref, sin_ref,
                     qn_ref, qr_ref, kn_ref, kr_ref, v_ref):
    def rms(x, g):
        return x * lax.rsqrt(jnp.mean(x * x, axis=-1, keepdims=True) + EPS) * g

    tm = cq_ref.shape[0]
    nope_w = MLA_HEADS * MLA_NOPE_DIM
    cq = rms(cq_ref[...].astype(F32), gqn_ref[...]).astype(BF16)
    q = jnp.dot(cq, wuq_ref[...], preferred_element_type=F32)
    ckv = rms(ckv_ref[...].astype(F32), gkvn_ref[...]).astype(BF16)
    kv = jnp.dot(ckv, wukv_ref[...], preferred_element_type=F32)
    v_ref[...] = kv[:, nope_w:].astype(BF16)

    lane = lax.broadcasted_iota(jnp.int32, (tm, LANES), 1)
    first = lane < MLA_ROPE_DIM
    low_half = (lane % MLA_ROPE_DIM) < (MLA_ROPE_DIM // 2)
    cosv = cos_ref[...]
    sinv = sin_ref[...]
    scale = (MLA_QK_DIM ** -0.5) * math.log2(math.e)
    inv_dim = 1.0 / MLA_QK_DIM

    def rope(x):
        swapped = jnp.where(low_half, pltpu.roll(x, LANES - MLA_ROPE_DIM // 2, 1),
                            pltpu.roll(x, MLA_ROPE_DIM // 2, 1))
        return x * cosv + swapped * sinv

    qgn, qgr = qgn_ref[...], qgr_ref[...]
    kgn, kgr = kgn_ref[...], kgr_ref[...]
    kraw = krope_ref[...].astype(F32)
    kdup = jnp.where(first, kraw, pltpu.roll(kraw, MLA_ROPE_DIM, 1))
    k_rope_sq = jnp.where(first, kraw * kraw, 0.0)
    k_roped = rope(kdup * kgr)

    for j in range(MLA_HEADS // 2):
        qr_j = q[:, nope_w + LANES * j:nope_w + LANES * (j + 1)]
        qr_sq = qr_j * qr_j
        q_rs, k_rs = [], []
        for hh in range(2):
            h = 2 * j + hh
            cols = slice(LANES * h, LANES * (h + 1))
            qn_h = q[:, cols]
            t = qn_h * qn_h + jnp.where(first if hh == 0 else jnp.logical_not(first), qr_sq, 0.0)
            rs = lax.rsqrt(jnp.sum(t, axis=-1, keepdims=True) * inv_dim + EPS)
            qn_ref[:, cols] = (qn_h * rs * qgn * scale).astype(BF16)
            q_rs.append(rs)
            kn_h = kv[:, cols]
            t = kn_h * kn_h + k_rope_sq
            rs = lax.rsqrt(jnp.sum(t, axis=-1, keepdims=True) * inv_dim + EPS)
            kn_ref[:, cols] = (kn_h * rs * kgn).astype(BF16)
            k_rs.append(rs)
        pair = slice(LANES * j, LANES * (j + 1))
        qr_ref[:, pair] = (rope(qr_j * jnp.where(first, q_rs[0], q_rs[1]) * qgr) * scale).astype(BF16)
        kr_ref[:, pair] = (k_roped * jnp.where(first, k_rs[0], k_rs[1])).astype(BF16)


def _mla_prep(proj, L, q_norm_g, kv_norm_g, w_uq, w_ukv, q_head_g, k_head_g):
    T = proj.shape[0]
    tm = min(512, L)
    H = MLA_HEADS
    wq = w_uq.reshape(MLA_Q_RANK, H, MLA_QK_DIM)
    wq = jnp.concatenate([wq[:, :, :MLA_NOPE_DIM].reshape(MLA_Q_RANK, -1),
                          wq[:, :, MLA_NOPE_DIM:].reshape(MLA_Q_RANK, -1)], axis=1).astype(BF16)
    wkv = w_ukv.reshape(MLA_KV_RANK, H, MLA_NOPE_DIM + MLA_V_DIM)
    wkv = jnp.concatenate([wkv[:, :, :MLA_NOPE_DIM].reshape(MLA_KV_RANK, -1),
                           wkv[:, :, MLA_NOPE_DIM:].reshape(MLA_KV_RANK, -1)], axis=1).astype(BF16)
    half = MLA_ROPE_DIM // 2
    freqs = ROPE_THETA ** (-jnp.arange(half, dtype=F32) / half)
    ang = jnp.arange(L).astype(F32)[:, None] * freqs[None, :]
    cos, sin = jnp.cos(ang), jnp.sin(ang)
    cos2 = jnp.tile(jnp.concatenate([cos, cos], axis=1), (1, 2))
    sin2 = jnp.tile(jnp.concatenate([-sin, sin], axis=1), (1, 2))

    def vec(g, reps=1):
        return jnp.tile(g.astype(F32), reps).reshape(1, -1)

    const = lambda shape: pl.BlockSpec(shape, lambda i: (0, 0))
    nblk = L // tm
    outs = pl.pallas_call(
        _mla_prep_kernel, grid=(T // tm,),
        in_specs=[
            pl.BlockSpec((tm, MLA_Q_RANK), lambda i: (i, COL_CQ // MLA_Q_RANK)),
            pl.BlockSpec((tm, MLA_KV_RANK), lambda i: (i, COL_CKV // MLA_KV_RANK)),
            pl.BlockSpec((tm, LANES), lambda i: (i, COL_KROPE // LANES)),
            const((1, MLA_Q_RANK)), const((1, MLA_KV_RANK)),
            const(wq.shape), const(wkv.shape),
            const((1, LANES)), const((1, LANES)), const((1, LANES)), const((1, LANES)),
            pl.BlockSpec((tm, LANES), lambda i: (i % nblk, 0)),
            pl.BlockSpec((tm, LANES), lambda i: (i % nblk, 0)),
        ],
        out_specs=[
            pl.BlockSpec((tm, H * MLA_NOPE_DIM), lambda i: (i, 0)),
            pl.BlockSpec((tm, H * MLA_ROPE_DIM), lambda i: (i, 0)),
            pl.BlockSpec((tm, H * MLA_NOPE_DIM), lambda i: (i, 0)),
            pl.BlockSpec((tm, H * MLA_ROPE_DIM), lambda i: (i, 0)),
            pl.BlockSpec((tm, H * MLA_V_DIM), lambda i: (i, 0)),
        ],
        out_shape=[
            jax.ShapeDtypeStruct((T, H * MLA_NOPE_DIM), BF16),
            jax.ShapeDtypeStruct((T, H * MLA_ROPE_DIM), BF16),
            jax.ShapeDtypeStruct((T, H * MLA_NOPE_DIM), BF16),
            jax.ShapeDtypeStruct((T, H * MLA_ROPE_DIM), BF16),
            jax.ShapeDtypeStruct((T, H * MLA_V_DIM), BF16),
        ],
        compiler_params=_params("parallel"), name="mla_prep")(
            proj, proj, proj, vec(q_norm_g), vec(kv_norm_g), wq, wkv,
            vec(q_head_g[:MLA_NOPE_DIM]), vec(q_head_g[MLA_NOPE_DIM:], 2),
            vec(k_head_g[:MLA_NOPE_DIM]), vec(k_head_g[MLA_NOPE_DIM:], 2), cos2, sin2)
    return outs


FLASH_STREAMS = 4


def _flash_kernel(qn_ref, qr_ref, kn_ref, kr_ref, v_ref, o_ref, q_sc, m_sc, acc_sc):
    h = pl.program_id(1)
    ki = pl.program_id(3)
    tq = qn_ref.shape[1]
    tk = kn_ref.shape[1]

    @pl.when(ki == 0)
    def _():
        lane = lax.broadcasted_iota(jnp.int32, (tq, LANES), 1)
        own = (lane // MLA_ROPE_DIM) == (h % 2)
        q_sc[:, :LANES] = qn_ref[0]
        q_sc[:, LANES:] = jnp.where(own, qr_ref[0], jnp.zeros_like(qr_ref[0]))
        m_sc[...] = jnp.full_like(m_sc, NEG_INF)
        acc_sc[...] = jnp.zeros_like(acc_sc)

    k = jnp.concatenate([kn_ref[0], kr_ref[0]], axis=-1)
    v_ones = jnp.concatenate([v_ref[0], jnp.ones((tk, LANES), BF16)], axis=-1)
    rows = tq // FLASH_STREAMS
    for r in range(FLASH_STREAMS):
        sl = slice(r * rows, (r + 1) * rows)
        s = lax.dot_general(q_sc[sl, :], k, (((1,), (1,)), ((), ())), preferred_element_type=F32)
        m_prev = m_sc[sl, :]
        m_new = jnp.maximum(m_prev, jnp.max(s, axis=-1, keepdims=True))
        p = jnp.exp2(s - m_new).astype(BF16)
        acc_sc[sl, :] = (jnp.exp2(m_prev - m_new) * acc_sc[sl, :]
                         + jnp.dot(p, v_ones, preferred_element_type=F32))
        m_sc[sl, :] = m_new

    @pl.when(ki == pl.num_programs(3) - 1)
    def _():
        o_ref[0] = (acc_sc[:, :LANES] / acc_sc[:, LANES:]).astype(o_ref.dtype)


def _flash_attention(qn, qr, kn, kr, v, B, L):
    H = MLA_HEADS
    tq = min(1024, L)
    tk = min(1024, L)
    r3 = lambda a: a.reshape(B, L, a.shape[-1])
    return pl.pallas_call(
        _flash_kernel, grid=(B, H, L // tq, L // tk),
        in_specs=[
            pl.BlockSpec((1, tq, LANES), lambda b, h, qi, ki: (b, qi, h)),
            pl.BlockSpec((1, tq, LANES), lambda b, h, qi, ki: (b, qi, h // 2)),
            pl.BlockSpec((1, tk, LANES), lambda b, h, qi, ki: (b, ki, h)),
            pl.BlockSpec((1, tk, LANES), lambda b, h, qi, ki: (b, ki, h // 2)),
            pl.BlockSpec((1, tk, LANES), lambda b, h, qi, ki: (b, ki, h)),
        ],
        out_specs=pl.BlockSpec((1, tq, LANES), lambda b, h, qi, ki: (b, qi, h)),
        out_shape=jax.ShapeDtypeStruct((B, L, H * MLA_V_DIM), BF16),
        scratch_shapes=[pltpu.VMEM((tq, 2 * LANES), BF16), pltpu.VMEM((tq, 1), F32),
                        pltpu.VMEM((tq, 2 * LANES), F32)],
        compiler_params=_params("parallel", "parallel", "parallel", "arbitrary"),
        name="mla_flash")(r3(qn), r3(qr), r3(kn), r3(kr), r3(v))


def _hyena_filters(L, w1, b1, w2, b2, w3, freq):
    t = jnp.linspace(0.0, 1.0, L, dtype=F32)[:, None]
    bands = jnp.linspace(1e-4, HY_POS_BANDS - 1, HY_POS_BANDS, dtype=F32)[None, :]
    w = 2.0 * math.pi * jnp.arange(L, dtype=F32)[:, None] / L
    z = jnp.concatenate([t, jnp.cos(bands * w), -jnp.sin(bands * w)], axis=-1)
    hp = lax.Precision.HIGHEST
    deltas = jnp.abs(jnp.linspace(HY_MIN_DECAY, HY_MAX_DECAY, HY_CH, dtype=F32))
    w3d = w3.astype(F32).reshape(w3.shape[0], 2, HY_ORDER * HY_CH)

    def direction(z, t, d):
        a = jnp.sin(freq[0] * (jnp.matmul(z, w1, precision=hp) + b1))
        a = jnp.sin(freq[1] * (jnp.matmul(a, w2, precision=hp) + b2))
        f = jnp.matmul(a, w3d[:, d], precision=hp).reshape(L, HY_ORDER, HY_CH)
        return f * jnp.exp(-t * deltas[None, :])[:, None, :]

    fwd = direction(z, t, 0)
    bwd_rev = direction(z[::-1], t[::-1], 1)[:L - 1]
    k = jnp.concatenate([fwd, jnp.zeros((1, HY_ORDER, HY_CH), F32), bwd_rev], axis=0)
    return k * lax.rsqrt(jnp.sum(k * k, axis=0, keepdims=True) + EPS)


HALO_ROWS = 16


def _short_conv_kernel(x_ref, xp_ref, xn_ref, w_ref, b_ref, o_ref, *, tiles_per_seq):
    i = pl.program_id(0)
    x = x_ref[...].astype(F32)
    tm = x.shape[0]
    row = lax.broadcasted_iota(jnp.int32, x.shape, 0)
    t = i % tiles_per_seq
    halo_prev = jnp.where(t == 0, 0.0, xp_ref[HALO_ROWS - 1:HALO_ROWS, :].astype(F32))
    halo_next = jnp.where(t == tiles_per_seq - 1, 0.0, xn_ref[0:1, :].astype(F32))
    prev = jnp.where(row == 0, halo_prev, pltpu.roll(x, 1, 0))
    nxt = jnp.where(row == tm - 1, halo_next, pltpu.roll(x, tm - 1, 0))
    o_ref[...] = prev * w_ref[0:1, :] + x * w_ref[1:2, :] + nxt * w_ref[2:3, :] + b_ref[...]


def _short_conv(proj, L, w, b):
    T = proj.shape[0]
    W = (HY_ORDER + 1) * HY_CH
    tm = min(1024, L)
    tc = 512
    c0 = COL_HY // tc
    hb = tm // HALO_ROWS
    n_halo = T // HALO_ROWS
    return pl.pallas_call(
        functools.partial(_short_conv_kernel, tiles_per_seq=L // tm),
        grid=(T // tm, W // tc),
        in_specs=[
            pl.BlockSpec((tm, tc), lambda i, c: (i, c0 + c)),
            pl.BlockSpec((HALO_ROWS, tc), lambda i, c: (jnp.maximum(i * hb - 1, 0), c0 + c)),
            pl.BlockSpec((HALO_ROWS, tc), lambda i, c: (jnp.minimum((i + 1) * hb, n_halo - 1), c0 + c)),
            pl.BlockSpec((3, tc), lambda i, c: (0, c)),
            pl.BlockSpec((1, tc), lambda i, c: (0, c)),
        ],
        out_specs=pl.BlockSpec((tm, tc), lambda i, c: (i, c)),
        out_shape=jax.ShapeDtypeStruct((T, W), F32),
        compiler_params=_params("parallel", "parallel"), name="hyena_short_conv")(
            proj, proj, proj, w.astype(F32), b.astype(F32).reshape(1, W))


HY_CB = LANES
HY_NLO = 128
HY_SLAB_CHUNK = 16
HY_UNROLL = 4


def _dft_tables(L):
    N = 2 * L
    nhi = N // HY_NLO
    kb = jnp.arange(nhi, dtype=jnp.int32)
    ang = (2.0 * math.pi / nhi) * ((kb[:, None] * kb[None, :]) % nhi).astype(F32)
    f1 = jnp.concatenate([jnp.cos(ang), -jnp.sin(ang)], axis=0)
    ka = jnp.arange(HY_NLO, dtype=jnp.int32)
    idx = (ka[None, None, :] * (ka[None, :, None] * nhi + kb[:, None, None])) % N
    ang = (2.0 * math.pi / N) * idx.astype(F32)
    gr, gi = jnp.cos(ang), -jnp.sin(ang)
    g = jnp.concatenate([jnp.concatenate([gr, -gi], axis=2),
                         jnp.concatenate([gi, gr], axis=2)], axis=1)
    return f1.astype(BF16), g.astype(BF16)


def _dft_major_stage(src_ref, f1, a_ref, n_rows):
    two_nhi = f1.shape[0]

    def body(n_lo, carry):
        xs = src_ref[pl.ds(n_lo, n_rows, stride=HY_NLO), :]
        dst = pl.multiple_of(n_lo * two_nhi, two_nhi)
        a_ref[pl.ds(dst, two_nhi), :] = jnp.dot(f1, xs.astype(BF16), preferred_element_type=F32)
        return carry

    lax.fori_loop(0, HY_NLO, body, 0, unroll=HY_UNROLL)


def _load_slab(a_ref, kb, nhi):
    re = a_ref[pl.ds(kb, HY_NLO, stride=2 * nhi), :]
    im = a_ref[pl.ds(nhi + kb, HY_NLO, stride=2 * nhi), :]
    return jnp.concatenate([re, im], axis=0).astype(BF16)


def _hyena_spectrum_kernel(k_ref, f1_ref, g_ref, o_ref, a_ref, *, nhi):
    j = pl.program_id(1)

    @pl.when(j == 0)
    def _():
        _dft_major_stage(k_ref, f1_ref[...], a_ref, nhi)

    def slab(s, carry):
        kb = j * HY_SLAB_CHUNK + s
        o_ref[0, s] = jnp.dot(g_ref[s], _load_slab(a_ref, kb, nhi), preferred_element_type=F32)
        return carry

    lax.fori_loop(0, HY_SLAB_CHUNK, slab, 0, unroll=HY_UNROLL)


def _hyena_spectrum(k2, f1, g):
    N, n_ch = k2.shape
    nhi = N // HY_NLO
    return pl.pallas_call(
        functools.partial(_hyena_spectrum_kernel, nhi=nhi),
        grid=(n_ch // HY_CB, nhi // HY_SLAB_CHUNK),
        in_specs=[
            pl.BlockSpec((N, HY_CB), lambda c, j: (0, c)),
            pl.BlockSpec(f1.shape, lambda c, j: (0, 0)),
            pl.BlockSpec((HY_SLAB_CHUNK, 2 * HY_NLO, 2 * HY_NLO), lambda c, j: (j, 0, 0)),
        ],
        out_specs=pl.BlockSpec((1, HY_SLAB_CHUNK, 2 * HY_NLO, HY_CB), lambda c, j: (c, j, 0, 0)),
        out_shape=jax.ShapeDtypeStruct((n_ch // HY_CB, nhi, 2 * HY_NLO, HY_CB), F32),
        scratch_shapes=[pltpu.VMEM((HY_NLO * 2 * nhi, HY_CB), F32)],
        compiler_params=_params("parallel", "arbitrary"), name="hyena_spectrum")(k2, f1, g)


def _hyena_conv_kernel(u_ref, gate_ref, bias_ref, f1_ref, f1t_ref, g_ref, kf_ref, o_ref,
                       a_ref, y_ref, *, nhi):
    j = pl.program_id(2)
    n_in = nhi // 2
    L = n_in * HY_NLO

    @pl.when(j == 0)
    def _():
        _dft_major_stage(u_ref.at[0], f1_ref[...], a_ref, n_in)

    def forward(s, carry):
        kb = j * HY_SLAB_CHUNK + s
        t = jnp.dot(g_ref[s], _load_slab(a_ref, kb, nhi), preferred_element_type=F32)
        kf = kf_ref[0, s]
        tr, ti = t[:HY_NLO], t[HY_NLO:]
        kr, ki = kf[:HY_NLO], kf[HY_NLO:]
        y_ref[s] = jnp.concatenate([tr * kr - ti * ki, tr * ki + ti * kr], axis=0).astype(BF16)
        return carry

    lax.fori_loop(0, HY_SLAB_CHUNK, forward, 0, unroll=HY_UNROLL)

    def inverse(s, carry):
        kb = j * HY_SLAB_CHUNK + s
        r = lax.dot_general(g_ref[s], y_ref[s], (((0,), (0,)), ((), ())),
                            preferred_element_type=F32)
        a_ref[pl.ds(kb, HY_NLO, stride=2 * nhi), :] = r[:HY_NLO]
        a_ref[pl.ds(nhi + kb, HY_NLO, stride=2 * nhi), :] = r[HY_NLO:]
        return carry

    lax.fori_loop(0, HY_SLAB_CHUNK, inverse, 0, unroll=HY_UNROLL)

    @pl.when(j == pl.num_programs(2) - 1)
    def _():
        f1t = f1t_ref[...]

        def body(n_lo, carry):
            src = pl.multiple_of(n_lo * 2 * nhi, 2 * nhi)
            blk = a_ref[pl.ds(src, 2 * nhi), :].astype(BF16)
            o_ref[0, pl.ds(n_lo, n_in, stride=HY_NLO), :] = jnp.dot(
                f1t, blk, preferred_element_type=F32)
            return carry

        lax.fori_loop(0, HY_NLO, body, 0, unroll=HY_UNROLL)
        inv_n = 1.0 / (2 * L)
        rows = 512

        def gate_rows(c, carry):
            r0 = pl.multiple_of(c * rows, rows)
            sl = pl.ds(r0, rows)
            o_ref[0, sl, :] = gate_ref[0, sl, :] * (o_ref[0, sl, :] * inv_n
                                                    + u_ref[0, sl, :] * bias_ref[...])
            return carry

        lax.fori_loop(0, L // rows, gate_rows, 0)


def _hyena_conv(u3, u_col, gate3, gate_col, bias, kf, kf_row, f1, g):
    B, L, _ = u3.shape
    nhi = 2 * L // HY_NLO
    f1_in = f1[:, :nhi // 2]
    n_cb = HY_CH // HY_CB
    return pl.pallas_call(
        functools.partial(_hyena_conv_kernel, nhi=nhi),
        grid=(B, n_cb, nhi // HY_SLAB_CHUNK),
        in_specs=[
            pl.BlockSpec((1, L, HY_CB), lambda b, c, j: (b, 0, u_col + c)),
            pl.BlockSpec((1, L, HY_CB), lambda b, c, j: (b, 0, gate_col + c)),
            pl.BlockSpec((1, HY_CB), lambda b, c, j: (0, c)),
            pl.BlockSpec(f1_in.shape, lambda b, c, j: (0, 0)),
            pl.BlockSpec(f1_in.shape[::-1], lambda b, c, j: (0, 0)),
            pl.BlockSpec((HY_SLAB_CHUNK, 2 * HY_NLO, 2 * HY_NLO), lambda b, c, j: (j, 0, 0)),
            pl.BlockSpec((1, HY_SLAB_CHUNK, 2 * HY_NLO, HY_CB), lambda b, c, j: (kf_row + c, j, 0, 0)),
        ],
        out_specs=pl.BlockSpec((1, L, HY_CB), lambda b, c, j: (b, 0, c)),
        out_shape=jax.ShapeDtypeStruct((B, L, HY_CH), F32),
        scratch_shapes=[pltpu.VMEM((HY_NLO * 2 * nhi, HY_CB), F32),
                        pltpu.VMEM((HY_SLAB_CHUNK, 2 * HY_NLO, HY_CB), BF16)],
        compiler_params=_params("parallel", "parallel", "arbitrary"), name="hyena_long_conv")(
            u3, gate3, bias.astype(F32).reshape(1, HY_CH), f1_in, f1_in.T, g, kf)


def _hyena(proj, B, L, short_w, short_b, w1, b1, w2, b2, w3, freq, bias, f1, g):
    u = _short_conv(proj, L, short_w, short_b).reshape(B, L, (HY_ORDER + 1) * HY_CH)
    k = _hyena_filters(L, w1, b1, w2, b2, w3, freq).reshape(2 * L, HY_ORDER * HY_CH)
    kf = _hyena_spectrum(k, f1, g)
    n_cb = HY_CH // HY_CB
    z = _hyena_conv(u, 0, u, n_cb, bias[0], kf, 0, f1, g)
    return _hyena_conv(z, 0, u, 2 * n_cb, bias[1], kf, n_cb, f1, g)


def _mix_norm_kernel(na_ref, mla_ref, hy_ref, g_ref, o_ref):
    def rms(x, g):
        return (x * lax.rsqrt(jnp.mean(x * x, axis=-1, keepdims=True) + EPS) * g).astype(BF16)

    a, b = NA_WIDTH, NA_WIDTH + MLA_HEADS * MLA_V_DIM
    o_ref[:, :a] = rms(na_ref[...].astype(F32), g_ref[:, :a])
    o_ref[:, a:b] = rms(mla_ref[...].astype(F32), g_ref[:, a:b])
    o_ref[:, b:] = rms(hy_ref[...].astype(F32), g_ref[:, b:])


def _mix_norm(o_na, o_mla, o_hy, gain):
    T = o_na.shape[0]
    W = o_na.shape[1] + o_mla.shape[1] + o_hy.shape[1]
    tm = min(1024, T)
    row = lambda a: pl.BlockSpec((tm, a.shape[1]), lambda i: (i, 0))
    return pl.pallas_call(
        _mix_norm_kernel, grid=(T // tm,),
        in_specs=[row(o_na), row(o_mla), row(o_hy), pl.BlockSpec((1, W), lambda i: (0, 0))],
        out_specs=pl.BlockSpec((tm, W), lambda i: (i, 0)),
        out_shape=jax.ShapeDtypeStruct((T, W), BF16),
        compiler_params=_params("parallel"), name="mix_norm")(
            o_na, o_mla, o_hy, gain.astype(F32).reshape(1, W))


def _ffn_kernel(te_ref, na_ref, h_ref, wg_ref, wu_ref, wd_ref, rw_ref, o_ref, acc_ref):
    i = pl.program_id(0)
    j = pl.program_id(1)
    last = pl.num_programs(1) - 1
    active = i < na_ref[0]

    @pl.when(jnp.logical_and(active, j == 0))
    def _():
        acc_ref[...] = jnp.zeros_like(acc_ref)

    @pl.when(active)
    def _():
        h = h_ref[...]
        g = jnp.dot(h, wg_ref[0], preferred_element_type=F32)
        u = jnp.dot(h, wu_ref[0], preferred_element_type=F32)
        a = (g * (1.0 / (1.0 + jnp.exp(-g))) * u).astype(BF16)
        acc_ref[...] += jnp.dot(a, wd_ref[0], preferred_element_type=F32)

    @pl.when(jnp.logical_and(active, j == last))
    def _():
        o_ref[...] = (acc_ref[...] * rw_ref[...]).astype(o_ref.dtype)

    @pl.when(jnp.logical_and(jnp.logical_not(active), j == last))
    def _():
        o_ref[...] = jnp.zeros_like(o_ref)


def _ffn(h, w_gate, w_up, w_down, tile_expert, n_active, row_weight, tm):
    P, D = h.shape
    F = w_gate.shape[2]
    tf = 512
    nf = F // tf

    def fidx(i, j, na):
        return jnp.where(i < na[0], j, nf - 1)

    grid_spec = pltpu.PrefetchScalarGridSpec(
        num_scalar_prefetch=2, grid=(P // tm, nf),
        in_specs=[
            pl.BlockSpec((tm, D), lambda i, j, te, na: (i, 0)),
            pl.BlockSpec((1, D, tf), lambda i, j, te, na: (te[i], 0, fidx(i, j, na))),
            pl.BlockSpec((1, D, tf), lambda i, j, te, na: (te[i], 0, fidx(i, j, na))),
            pl.BlockSpec((1, tf, D), lambda i, j, te, na: (te[i], fidx(i, j, na), 0)),
            pl.BlockSpec((tm, 1), lambda i, j, te, na: (i, 0)),
        ],
        out_specs=pl.BlockSpec((tm, D), lambda i, j, te, na: (i, 0)),
        scratch_shapes=[pltpu.VMEM((tm, D), F32)])
    return pl.pallas_call(
        _ffn_kernel, grid_spec=grid_spec,
        out_shape=jax.ShapeDtypeStruct((P, D), BF16),
        compiler_params=_params("parallel", "arbitrary"), name="swiglu_ffn")(
            tile_expert, n_active, h, w_gate, w_up, w_down, row_weight)


def _dense_ffn(h, w_gate, w_up, w_down):
    T = h.shape[0]
    tm = min(1024, T)
    n = T // tm
    return _ffn(h, w_gate[None].astype(BF16), w_up[None].astype(BF16), w_down[None].astype(BF16),
                jnp.zeros((n,), jnp.int32), jnp.full((1,), n, jnp.int32),
                jnp.ones((T, 1), F32), tm)


def _moe_ffn(h, logits, w_gate, w_up, w_down):
    T, D = h.shape
    E = N_EXPERTS
    tm = min(1024, T)
    probs = jax.nn.softmax(logits[:, :E], axis=-1)
    top_p, top_i = lax.top_k(probs, TOP_K)
    top_p = top_p / jnp.sum(top_p, axis=-1, keepdims=True)
    flat_e = top_i.reshape(-1).astype(jnp.int32)
    n_slots = T * TOP_K
    order = jnp.argsort(flat_e, stable=True).astype(jnp.int32)
    counts = jnp.sum(flat_e[:, None] == jnp.arange(E, dtype=jnp.int32)[None, :], axis=0).astype(jnp.int32)
    tiles_per = (counts + tm - 1) // tm
    tile_end = jnp.cumsum(tiles_per)
    row_start = (tile_end - tiles_per) * tm
    slot_start = jnp.cumsum(counts) - counts
    sorted_e = flat_e[order]
    dest_sorted = row_start[sorted_e] + jnp.arange(n_slots, dtype=jnp.int32) - slot_start[sorted_e]
    n_tiles = n_slots // tm + E
    P = n_tiles * tm
    src_token = jnp.zeros((P,), jnp.int32).at[dest_sorted].set(order // TOP_K)
    row_weight = jnp.zeros((P,), F32).at[dest_sorted].set(top_p.reshape(-1)[order])
    dest = jnp.zeros((n_slots,), jnp.int32).at[order].set(dest_sorted)
    n_active = tile_end[-1:].astype(jnp.int32)
    tile_ids = jnp.arange(n_tiles, dtype=jnp.int32)
    tile_expert = jnp.sum(tile_ids[:, None] >= tile_end[None, :], axis=1).astype(jnp.int32)
    tile_expert = jnp.minimum(tile_expert, tile_expert[jnp.maximum(n_active[0] - 1, 0)])
    hs = jnp.take(h, src_token, axis=0)
    y = _ffn(hs, w_gate.astype(BF16), w_up.astype(BF16), w_down.astype(BF16),
             tile_expert, n_active, row_weight.reshape(P, 1), tm)
    dest = dest.reshape(T, TOP_K)
    return jnp.take(y, dest[:, 0], axis=0), jnp.take(y, dest[:, 1], axis=0)


def kernel(x, attn_norm_g, w_in, na_q_g, na_k_g, na_rpb, mla_q_norm_g, mla_kv_norm_g, mla_w_uq, mla_w_ukv, mla_q_g, mla_k_g, hy_short_w, hy_short_b, hy_w1, hy_b1, hy_w2, hy_b2, hy_w3, hy_freq, hy_bias, group_norm_g, w_out, ffn_norm_g, dense_w_gate, dense_w_up, dense_w_down, router_w, moe_w_gate, moe_w_up, moe_w_down):
    B, L, D = x.shape
    T = B * L
    depth = attn_norm_g.shape[0]
    assert L % (NA_K_ROWS * GRID_W) == 0 and w_in.shape[2] == IN_WIDTH
    x2 = x.reshape(T, D).astype(F32)
    f1, g_dft = _dft_tables(L)
    deltas = []
    for l in range(depth):
        x2, h = _add_norm(x2, deltas, attn_norm_g[l].astype(F32))
        w = w_in[l]
        w_in_p = jnp.concatenate(
            [w[:, :SRC_CKV], w[:, SRC_HY:], w[:, SRC_CKV:SRC_HY],
             jnp.zeros((D, IN_WIDTH_PAD - IN_WIDTH), w.dtype)], axis=1).astype(BF16)
        proj = _matmul(h, w_in_p, BF16, 1024)
        o_na = _neighborhood_attention(proj.reshape(B, L, IN_WIDTH_PAD), na_rpb[l],
                                       na_q_g[l], na_k_g[l]).reshape(T, NA_WIDTH)
        qn, qr, kn, kr, v = _mla_prep(proj, L, mla_q_norm_g[l], mla_kv_norm_g[l], mla_w_uq[l],
                                      mla_w_ukv[l], mla_q_g[l], mla_k_g[l])
        o_mla = _flash_attention(qn, qr, kn, kr, v, B, L).reshape(T, MLA_HEADS * MLA_V_DIM)
        o_hy = _hyena(proj, B, L, hy_short_w[l], hy_short_b[l], hy_w1[l], hy_b1[l], hy_w2[l],
                      hy_b2[l], hy_w3[l], hy_freq[l], hy_bias[l], f1, g_dft).reshape(T, HY_CH)
        mix = _mix_norm(o_na, o_mla, o_hy, group_norm_g[l])
        d_mix = _matmul(mix, w_out[l].astype(BF16), BF16, 1024)
        i = l // 2
        if l % 2 == 0:
            x2, h = _add_norm(x2, [d_mix], ffn_norm_g[l].astype(F32))
            deltas = [_dense_ffn(h, dense_w_gate[i], dense_w_up[i], dense_w_down[i])]
        else:
            wr = jnp.pad(router_w[i].astype(F32), ((0, 0), (0, LANES - N_EXPERTS)))
            x2, h, logits = _add_norm(x2, [d_mix], ffn_norm_g[l].astype(F32), wr)
            deltas = list(_moe_ffn(h, logits, moe_w_gate[i], moe_w_up[i], moe_w_down[i]))
    (x2,) = _add_norm(x2, deltas)
    return x2.reshape(B, L, D).astype(x.dtype)
```

```python
import functools
import math

import jax
import jax.numpy as jnp
import numpy as np
from jax import lax
from jax.experimental import pallas as pl
from jax.experimental.pallas import tpu as pltpu

F32 = jnp.float32
BF16 = jnp.bfloat16

GRID_W = 64
NA_HEADS = 8
NA_HEAD_DIM = 64
NA_WIDTH = NA_HEADS * NA_HEAD_DIM
NA_WIN_ROWS = 8
NA_WIN_COLS = 16
MLA_HEADS = 8
MLA_NOPE_DIM = 128
MLA_ROPE_DIM = 64
MLA_V_DIM = 128
MLA_QK_DIM = MLA_NOPE_DIM + MLA_ROPE_DIM
MLA_Q_RANK = 512
MLA_KV_RANK = 256
ROPE_THETA = 10000.0
HY_CH = 512
HY_ORDER = 2
HY_POS_BANDS = 16
HY_DECAY_TARGET = 1e-2
HY_DECAY_FAST = 0.3
HY_DECAY_SLOW = 1.5
HY_MAX_DECAY = math.log(HY_DECAY_TARGET) / HY_DECAY_FAST
HY_MIN_DECAY = math.log(HY_DECAY_TARGET) / HY_DECAY_SLOW
N_EXPERTS = 8
TOP_K = 2
EPS = 1e-6
NEG_INF = -1e30

SRC_KROPE = 3 * NA_WIDTH + MLA_Q_RANK + MLA_KV_RANK
SRC_HY = SRC_KROPE + MLA_ROPE_DIM
IN_WIDTH = SRC_HY + (HY_ORDER + 1) * HY_CH
SRC_CKV = 3 * NA_WIDTH + MLA_Q_RANK
COL_NA_Q = 0
COL_NA_K = NA_WIDTH
COL_NA_V = 2 * NA_WIDTH
COL_CQ = 3 * NA_WIDTH
COL_HY = COL_CQ + MLA_Q_RANK
COL_CKV = COL_HY + (HY_ORDER + 1) * HY_CH
COL_KROPE = COL_CKV + MLA_KV_RANK
LANES = 128
IN_WIDTH_PAD = 4096

VMEM_LIMIT_BYTES = 56 * 1024 * 1024

NA_Q_ROWS = 8
NA_K_ROWS = 16


def _params(*sem):
    return pltpu.CompilerParams(dimension_semantics=sem, vmem_limit_bytes=VMEM_LIMIT_BYTES)


def _add_norm_kernel(*refs, n_delta, with_router, with_norm):
    x_ref = refs[0]
    d_refs = refs[1:1 + n_delta]
    pos = 1 + n_delta
    x = x_ref[...]
    for d in d_refs:
        x = x + d[...].astype(F32)
    if not with_norm:
        refs[pos][...] = x
        return
    g_ref = refs[pos]
    pos += 1
    if with_router:
        wr_ref = refs[pos]
        pos += 1
    xo_ref, h_ref = refs[pos], refs[pos + 1]
    xo_ref[...] = x
    h = x * lax.rsqrt(jnp.mean(x * x, axis=-1, keepdims=True) + EPS) * g_ref[...]
    h_ref[...] = h.astype(BF16)
    if with_router:
        refs[pos + 2][...] = jnp.dot(h, wr_ref[...], preferred_element_type=F32,
                                     precision=lax.Precision.HIGHEST)


def _add_norm(x, deltas, gain=None, router_w=None):
    T, D = x.shape
    tm = min(512, T)
    with_norm = gain is not None
    with_router = router_w is not None
    row = pl.BlockSpec((tm, D), lambda i: (i, 0))
    in_specs = [row] + [row] * len(deltas)
    args = [x] + list(deltas)
    out_shape = [jax.ShapeDtypeStruct((T, D), F32)]
    out_specs = [row]
    if with_norm:
        in_specs.append(pl.BlockSpec((1, D), lambda i: (0, 0)))
        args.append(gain.reshape(1, D))
        if with_router:
            in_specs.append(pl.BlockSpec((D, LANES), lambda i: (0, 0)))
            args.append(router_w)
        out_shape.append(jax.ShapeDtypeStruct((T, D), BF16))
        out_specs.append(row)
        if with_router:
            out_shape.append(jax.ShapeDtypeStruct((T, LANES), F32))
            out_specs.append(pl.BlockSpec((tm, LANES), lambda i: (i, 0)))
    out = pl.pallas_call(
        functools.partial(_add_norm_kernel, n_delta=len(deltas), with_router=with_router,
                          with_norm=with_norm),
        grid=(T // tm,), in_specs=in_specs, out_specs=out_specs, out_shape=out_shape,
        compiler_params=_params("parallel"), name="add_norm")(*args)
    return out


def _mm_kernel(a_ref, b_ref, o_ref):
    o_ref[...] = jnp.dot(a_ref[...], b_ref[...], preferred_element_type=F32).astype(o_ref.dtype)


def _matmul(a, b, out_dtype, tn):
    M, K = a.shape
    N = b.shape[1]
    tm = min(1024, M)
    return pl.pallas_call(
        _mm_kernel, grid=(M // tm, N // tn),
        in_specs=[pl.BlockSpec((tm, K), lambda i, j: (i, 0)),
                  pl.BlockSpec((K, tn), lambda i, j: (0, j))],
        out_specs=pl.BlockSpec((tm, tn), lambda i, j: (i, j)),
        out_shape=jax.ShapeDtypeStruct((M, N), out_dtype),
        compiler_params=_params("parallel", "parallel"), name="matmul")(a, b)


def _na_bias_table(rpb, rows):
    n_groups = rows // NA_Q_ROWS
    reps = (0, min(1, n_groups - 1), n_groups - 1)
    row_off = np.zeros((3, NA_Q_ROWS, NA_K_ROWS), np.int32)
    row_ok = np.zeros((3, NA_Q_ROWS, NA_K_ROWS), bool)
    for p, g in enumerate(reps):
        start = int(np.clip(g * NA_Q_ROWS - NA_WIN_ROWS // 2, 0, rows - NA_K_ROWS))
        for i in range(NA_Q_ROWS):
            r = g * NA_Q_ROWS + i
            rs = int(np.clip(r - NA_WIN_ROWS // 2, 0, rows - NA_WIN_ROWS))
            for j in range(NA_K_ROWS):
                kr = start + j
                ok = rs <= kr < rs + NA_WIN_ROWS
                row_ok[p, i, j] = ok
                row_off[p, i, j] = np.clip(kr - r + NA_WIN_ROWS - 1, 0, 2 * NA_WIN_ROWS - 2)
    cols = np.arange(GRID_W)
    col_start = np.clip(cols - NA_WIN_COLS // 2, 0, GRID_W - NA_WIN_COLS)
    col_ok = (cols[None, :] >= col_start[:, None]) & (cols[None, :] < col_start[:, None] + NA_WIN_COLS)
    col_off = np.clip(cols[None, :] - cols[:, None] + NA_WIN_COLS - 1, 0, 2 * NA_WIN_COLS - 2)
    row_sel = np.eye(2 * NA_WIN_ROWS - 1, dtype=np.float32)[row_off]
    col_sel = np.eye(2 * NA_WIN_COLS - 1, dtype=np.float32)[col_off]
    tab = jnp.einsum('hrc,pijr,abc->hpiajb', rpb.astype(F32), row_sel, col_sel,
                     precision=lax.Precision.HIGHEST)
    ok = row_ok[:, :, None, :, None] & col_ok[None, None, :, None, :]
    tab = jnp.where(ok[None], tab, NEG_INF)
    return tab.reshape(NA_HEADS, 3, NA_Q_ROWS * GRID_W, NA_K_ROWS * GRID_W)


def _na_kernel(q_ref, k_ref, v_ref, bias_ref, gq_ref, gk_ref, ones_ref, o_ref, *, rows):
    g = pl.program_id(2)
    tq = NA_Q_ROWS * GRID_W
    tk = NA_K_ROWS * GRID_W
    start = jnp.clip(g * NA_Q_ROWS - NA_WIN_ROWS // 2, 0, rows - NA_K_ROWS) * GRID_W
    start = pl.multiple_of(start, GRID_W)
    ones_bd = ones_ref[...]

    def head_norm(x, gain):
        x2 = x * x
        hi = x2.astype(BF16)
        lo = (x2 - hi.astype(F32)).astype(BF16)
        ssq = (jnp.dot(hi, ones_bd, preferred_element_type=F32)
               + jnp.dot(lo, ones_bd, preferred_element_type=F32))
        return x * lax.rsqrt(ssq * (1.0 / NA_HEAD_DIM) + EPS) * gain

    q = head_norm(q_ref[0].astype(F32), gq_ref[...]) * (NA_HEAD_DIM ** -0.5)
    kw = head_norm(k_ref[0, pl.ds(start, tk), :].astype(F32), gk_ref[...]).astype(BF16)
    vw = v_ref[0, pl.ds(start, tk), :]
    lane = lax.broadcasted_iota(jnp.int32, (tq, LANES), 1)
    outs = []
    for hh in range(2):
        sel = (lane < NA_HEAD_DIM) if hh == 0 else (lane >= NA_HEAD_DIM)
        qm = jnp.where(sel, q, 0.0).astype(BF16)
        s = lax.dot_general(qm, kw, (((1,), (1,)), ((), ())), preferred_element_type=F32)
        s = s + bias_ref[hh, 0]
        m = jnp.max(s, axis=-1, keepdims=True)
        p = jnp.exp(s - m)
        l = jnp.sum(p, axis=-1, keepdims=True)
        o = jnp.dot(p.astype(BF16), vw, preferred_element_type=F32)
        outs.append(o / l)
    o_ref[0] = jnp.where(lane < NA_HEAD_DIM, outs[0], outs[1]).astype(o_ref.dtype)


def _neighborhood_attention(proj3, rpb, q_g, k_g):
    B, L, _ = proj3.shape
    rows = L // GRID_W
    n_groups = rows // NA_Q_ROWS
    tq = NA_Q_ROWS * GRID_W
    tk = NA_K_ROWS * GRID_W
    bias = _na_bias_table(rpb, rows)
    gq = jnp.tile(q_g.astype(F32), 2).reshape(1, LANES)
    gk = jnp.tile(k_g.astype(F32), 2).reshape(1, LANES)
    ones_bd = jnp.asarray(np.kron(np.eye(2), np.ones((NA_HEAD_DIM, NA_HEAD_DIM))), BF16)
    qb, kb, vb = COL_NA_Q // LANES, COL_NA_K // LANES, COL_NA_V // LANES

    def pattern(g):
        return (g > 0).astype(jnp.int32) + (g == n_groups - 1).astype(jnp.int32)

    return pl.pallas_call(
        functools.partial(_na_kernel, rows=rows),
        grid=(NA_HEADS // 2, B, n_groups),
        in_specs=[
            pl.BlockSpec((1, tq, LANES), lambda hp, b, g: (b, g, qb + hp)),
            pl.BlockSpec((1, L, LANES), lambda hp, b, g: (b, 0, kb + hp)),
            pl.BlockSpec((1, L, LANES), lambda hp, b, g: (b, 0, vb + hp)),
            pl.BlockSpec((2, 1, tq, tk), lambda hp, b, g: (hp, pattern(g), 0, 0)),
            pl.BlockSpec((1, LANES), lambda hp, b, g: (0, 0)),
            pl.BlockSpec((1, LANES), lambda hp, b, g: (0, 0)),
            pl.BlockSpec((LANES, LANES), lambda hp, b, g: (0, 0)),
        ],
        out_specs=pl.BlockSpec((1, tq, LANES), lambda hp, b, g: (b, g, hp)),
        out_shape=jax.ShapeDtypeStruct((B, L, NA_WIDTH), BF16),
        compiler_params=_params("parallel", "parallel", "arbitrary"),
        name="neighborhood_attention")(proj3, proj3, proj3, bias, gq, gk, ones_bd)


def _mla_prep_kernel(cq_ref, ckv_ref, krope_ref, gqn_ref, gkvn_ref, wuq_ref, wukv_ref,
                     qgn_ref, qgr_ref, kgn_ref, kgr_ref, cos_ref, sin_ref,
                     qn_ref, qr_ref, kn_ref, kr_ref, v_ref):
    def rms(x, g):
        return x * lax.rsqrt(jnp.mean(x * x, axis=-1, keepdims=True) + EPS) * g

    tm = cq_ref.shape[0]
    nope_w = MLA_HEADS * MLA_NOPE_DIM
    cq = rms(cq_ref[...].astype(F32), gqn_ref[...]).astype(BF16)
    q = jnp.dot(cq, wuq_ref[...], preferred_element_type=F32)
    ckv = rms(ckv_ref[...].astype(F32), gkvn_ref[...]).astype(BF16)
    kv = jnp.dot(ckv, wukv_ref[...], preferred_element_type=F32)
    v_ref[...] = kv[:, nope_w:].astype(BF16)

    lane = lax.broadcasted_iota(jnp.int32, (tm, LANES), 1)
    first = lane < MLA_ROPE_DIM
    low_half = (lane % MLA_ROPE_DIM) < (MLA_ROPE_DIM // 2)
    cosv = cos_ref[...]
    sinv = sin_ref[...]
    scale = (MLA_QK_DIM ** -0.5) * math.log2(math.e)
    inv_dim = 1.0 / MLA_QK_DIM

    def rope(x):
        swapped = jnp.where(low_half, pltpu.roll(x, LANES - MLA_ROPE_DIM // 2, 1),
                            pltpu.roll(x, MLA_ROPE_DIM // 2, 1))
        return x * cosv + swapped * sinv

    qgn, qgr = qgn_ref[...], qgr_ref[...]
    kgn, kgr = kgn_ref[...], kgr_ref[...]
    kraw = krope_ref[...].astype(F32)
    kdup = jnp.where(first, kraw, pltpu.roll(kraw, MLA_ROPE_DIM, 1))
    k_rope_sq = jnp.where(first, kraw * kraw, 0.0)
    k_roped = rope(kdup * kgr)

    for j in range(MLA_HEADS // 2):
        qr_j = q[:, nope_w + LANES * j:nope_w + LANES * (j + 1)]
        qr_sq = qr_j * qr_j
        q_rs, k_rs = [], []
        for hh in range(2):
            h = 2 * j + hh
            cols = slice(LANES * h, LANES * (h + 1))
            qn_h = q[:, cols]
            t = qn_h * qn_h + jnp.where(first if hh == 0 else jnp.logical_not(first), qr_sq, 0.0)
            rs = lax.rsqrt(jnp.sum(t, axis=-1, keepdims=True) * inv_dim + EPS)
            qn_ref[:, cols] = (qn_h * rs * qgn * scale).astype(BF16)
            q_rs.append(rs)
            kn_h = kv[:, cols]
            t = kn_h * kn_h + k_rope_sq
            rs = lax.rsqrt(jnp.sum(t, axis=-1, keepdims=True) * inv_dim + EPS)
            kn_ref[:, cols] = (kn_h * rs * kgn).astype(BF16)
            k_rs.append(rs)
        pair = slice(LANES * j, LANES * (j + 1))
        qr_ref[:, pair] = (rope(qr_j * jnp.where(first, q_rs[0], q_rs[1]) * qgr) * scale).astype(BF16)
        kr_ref[:, pair] = (k_roped * jnp.where(first, k_rs[0], k_rs[1])).astype(BF16)


def _mla_prep(proj, L, q_norm_g, kv_norm_g, w_uq, w_ukv, q_head_g, k_head_g):
    T = proj.shape[0]
    tm = min(512, L)
    H = MLA_HEADS
    wq = w_uq.reshape(MLA_Q_RANK, H, MLA_QK_DIM)
    wq = jnp.concatenate([wq[:, :, :MLA_NOPE_DIM].reshape(MLA_Q_RANK, -1),
                          wq[:, :, MLA_NOPE_DIM:].reshape(MLA_Q_RANK, -1)], axis=1).astype(BF16)
    wkv = w_ukv.reshape(MLA_KV_RANK, H, MLA_NOPE_DIM + MLA_V_DIM)
    wkv = jnp.concatenate([wkv[:, :, :MLA_NOPE_DIM].reshape(MLA_KV_RANK, -1),
                           wkv[:, :, MLA_NOPE_DIM:].reshape(MLA_KV_RANK, -1)], axis=1).astype(BF16)
    half = MLA_ROPE_DIM // 2
    freqs = ROPE_THETA ** (-jnp.arange(half, dtype=F32) / half)
    ang = jnp.arange(L).astype(F32)[:, None] * freqs[None, :]
    cos, sin = jnp.cos(ang), jnp.sin(ang)
    cos2 = jnp.tile(jnp.concatenate([cos, cos], axis=1), (1, 2))
    sin2 = jnp.tile(jnp.concatenate([-sin, sin], axis=1), (1, 2))

    def vec(g, reps=1):
        return jnp.tile(g.astype(F32), reps).reshape(1, -1)

    const = lambda shape: pl.BlockSpec(shape, lambda i: (0, 0))
    nblk = L // tm
    outs = pl.pallas_call(
        _mla_prep_kernel, grid=(T // tm,),
        in_specs=[
            pl.BlockSpec((tm, MLA_Q_RANK), lambda i: (i, COL_CQ // MLA_Q_RANK)),
            pl.BlockSpec((tm, MLA_KV_RANK), lambda i: (i, COL_CKV // MLA_KV_RANK)),
            pl.BlockSpec((tm, LANES), lambda i: (i, COL_KROPE // LANES)),
            const((1, MLA_Q_RANK)), const((1, MLA_KV_RANK)),
            const(wq.shape), const(wkv.shape),
            const((1, LANES)), const((1, LANES)), const((1, LANES)), const((1, LANES)),
            pl.BlockSpec((tm, LANES), lambda i: (i % nblk, 0)),
            pl.BlockSpec((tm, LANES), lambda i: (i % nblk, 0)),
        ],
        out_specs=[
            pl.BlockSpec((tm, H * MLA_NOPE_DIM), lambda i: (i, 0)),
            pl.BlockSpec((tm, H * MLA_ROPE_DIM), lambda i: (i, 0)),
            pl.BlockSpec((tm, H * MLA_NOPE_DIM), lambda i: (i, 0)),
            pl.BlockSpec((tm, H * MLA_ROPE_DIM), lambda i: (i, 0)),
            pl.BlockSpec((tm, H * MLA_V_DIM), lambda i: (i, 0)),
        ],
        out_shape=[
            jax.ShapeDtypeStruct((T, H * MLA_NOPE_DIM), BF16),
            jax.ShapeDtypeStruct((T, H * MLA_ROPE_DIM), BF16),
            jax.ShapeDtypeStruct((T, H * MLA_NOPE_DIM), BF16),
            jax.ShapeDtypeStruct((T, H * MLA_ROPE_DIM), BF16),
            jax.ShapeDtypeStruct((T, H * MLA_V_DIM), BF16),
        ],
        compiler_params=_params("parallel"), name="mla_prep")(
            proj, proj, proj, vec(q_norm_g), vec(kv_norm_g), wq, wkv,
            vec(q_head_g[:MLA_NOPE_DIM]), vec(q_head_g[MLA_NOPE_DIM:], 2),
            vec(k_head_g[:MLA_NOPE_DIM]), vec(k_head_g[MLA_NOPE_DIM:], 2), cos2, sin2)
    return outs


FLASH_STREAMS = 4


def _flash_kernel(qn_ref, qr_ref, kn0_ref, kr0_ref, kn1_ref, kr1_ref, v_ref, o_ref,
                  q_sc, s_even, s_odd, m_sc, acc_sc):
    h = pl.program_id(1)
    ki = pl.program_id(3)
    tq = qn_ref.shape[1]
    tk = kn0_ref.shape[1]
    rows = tq // FLASH_STREAMS
    nt = (((1,), (1,)), ((), ()))

    @pl.when(ki == 0)
    def _():
        lane = lax.broadcasted_iota(jnp.int32, (tq, LANES), 1)
        own = (lane // MLA_ROPE_DIM) == (h % 2)
        q_sc[:, :LANES] = qn_ref[0]
        q_sc[:, LANES:] = jnp.where(own, qr_ref[0], jnp.zeros_like(qr_ref[0]))
        m_sc[...] = jnp.full_like(m_sc, NEG_INF)
        acc_sc[...] = jnp.zeros_like(acc_sc)
        k0 = jnp.concatenate([kn0_ref[0], kr0_ref[0]], axis=-1)
        for r in range(FLASH_STREAMS):
            sl = slice(r * rows, (r + 1) * rows)
            s_even[sl, :] = lax.dot_general(q_sc[sl, :], k0, nt, preferred_element_type=F32)

    def step(s_cur, s_next):
        k1 = jnp.concatenate([kn1_ref[0], kr1_ref[0]], axis=-1)
        v_ones = jnp.concatenate([v_ref[0], jnp.ones((tk, LANES), BF16)], axis=-1)
        for r in range(FLASH_STREAMS):
            sl = slice(r * rows, (r + 1) * rows)
            s_next[sl, :] = lax.dot_general(q_sc[sl, :], k1, nt, preferred_element_type=F32)
            s = s_cur[sl, :]
            m_prev = m_sc[sl, :]
            m_new = jnp.maximum(m_prev, jnp.max(s, axis=-1, keepdims=True))
            p = jnp.exp2(s - m_new).astype(BF16)
            acc_sc[sl, :] = (jnp.exp2(m_prev - m_new) * acc_sc[sl, :]
                             + jnp.dot(p, v_ones, preferred_element_type=F32))
            m_sc[sl, :] = m_new

    @pl.when(ki % 2 == 0)
    def _():
        step(s_even, s_odd)

    @pl.when(ki % 2 == 1)
    def _():
        step(s_odd, s_even)

    @pl.when(ki == pl.num_programs(3) - 1)
    def _():
        o_ref[0] = (acc_sc[:, :LANES] / acc_sc[:, LANES:]).astype(o_ref.dtype)


def _flash_attention(qn, qr, kn, kr, v, B, L):
    H = MLA_HEADS
    tq = min(1024, L)
    tk = min(1024, L)
    r3 = lambda a: a.reshape(B, L, a.shape[-1])
    nk = L // tk
    ahead = lambda ki: jnp.minimum(ki + 1, nk - 1)
    return pl.pallas_call(
        _flash_kernel, grid=(B, H, L // tq, nk),
        in_specs=[
            pl.BlockSpec((1, tq, LANES), lambda b, h, qi, ki: (b, qi, h)),
            pl.BlockSpec((1, tq, LANES), lambda b, h, qi, ki: (b, qi, h // 2)),
            pl.BlockSpec((1, tk, LANES), lambda b, h, qi, ki: (b, 0, h)),
            pl.BlockSpec((1, tk, LANES), lambda b, h, qi, ki: (b, 0, h // 2)),
            pl.BlockSpec((1, tk, LANES), lambda b, h, qi, ki: (b, ahead(ki), h)),
            pl.BlockSpec((1, tk, LANES), lambda b, h, qi, ki: (b, ahead(ki), h // 2)),
            pl.BlockSpec((1, tk, LANES), lambda b, h, qi, ki: (b, ki, h)),
        ],
        out_specs=pl.BlockSpec((1, tq, LANES), lambda b, h, qi, ki: (b, qi, h)),
        out_shape=jax.ShapeDtypeStruct((B, L, H * MLA_V_DIM), BF16),
        scratch_shapes=[pltpu.VMEM((tq, 2 * LANES), BF16), pltpu.VMEM((tq, tk), F32),
                        pltpu.VMEM((tq, tk), F32), pltpu.VMEM((tq, 1), F32),
                        pltpu.VMEM((tq, 2 * LANES), F32)],
        compiler_params=_params("parallel", "parallel", "parallel", "arbitrary"),
        name="mla_flash")(r3(qn), r3(qr), r3(kn), r3(kr), r3(kn), r3(kr), r3(v))


def _hyena_filters(L, w1, b1, w2, b2, w3, freq):
    t = jnp.linspace(0.0, 1.0, L, dtype=F32)[:, None]
    bands = jnp.linspace(1e-4, HY_POS_BANDS - 1, HY_POS_BANDS, dtype=F32)[None, :]
    w = 2.0 * math.pi * jnp.arange(L, dtype=F32)[:, None] / L
    z = jnp.concatenate([t, jnp.cos(bands * w), -jnp.sin(bands * w)], axis=-1)
    hp = lax.Precision.HIGHEST
    deltas = jnp.abs(jnp.linspace(HY_MIN_DECAY, HY_MAX_DECAY, HY_CH, dtype=F32))
    w3d = w3.astype(F32).reshape(w3.shape[0], 2, HY_ORDER * HY_CH)

    def direction(z, t, d):
        a = jnp.sin(freq[0] * (jnp.matmul(z, w1, precision=hp) + b1))
        a = jnp.sin(freq[1] * (jnp.matmul(a, w2, precision=hp) + b2))
        f = jnp.matmul(a, w3d[:, d], precision=hp).reshape(L, HY_ORDER, HY_CH)
        return f * jnp.exp(-t * deltas[None, :])[:, None, :]

    fwd = direction(z, t, 0)
    bwd_rev = direction(z[::-1], t[::-1], 1)[:L - 1]
    k = jnp.concatenate([fwd, jnp.zeros((1, HY_ORDER, HY_CH), F32), bwd_rev], axis=0)
    return k * lax.rsqrt(jnp.sum(k * k, axis=0, keepdims=True) + EPS)


HALO_ROWS = 16


def _short_conv_kernel(x_ref, xp_ref, xn_ref, w_ref, b_ref, o_ref, *, tiles_per_seq):
    i = pl.program_id(0)
    x = x_ref[...].astype(F32)
    tm = x.shape[0]
    row = lax.broadcasted_iota(jnp.int32, x.shape, 0)
    t = i % tiles_per_seq
    halo_prev = jnp.where(t == 0, 0.0, xp_ref[HALO_ROWS - 1:HALO_ROWS, :].astype(F32))
    halo_next = jnp.where(t == tiles_per_seq - 1, 0.0, xn_ref[0:1, :].astype(F32))
    prev = jnp.where(row == 0, halo_prev, pltpu.roll(x, 1, 0))
    nxt = jnp.where(row == tm - 1, halo_next, pltpu.roll(x, tm - 1, 0))
    o_ref[...] = prev * w_ref[0:1, :] + x * w_ref[1:2, :] + nxt * w_ref[2:3, :] + b_ref[...]


def _short_conv(proj, L, w, b):
    T = proj.shape[0]
    W = (HY_ORDER + 1) * HY_CH
    tm = min(1024, L)
    tc = 512
    c0 = COL_HY // tc
    hb = tm // HALO_ROWS
    n_halo = T // HALO_ROWS
    return pl.pallas_call(
        functools.partial(_short_conv_kernel, tiles_per_seq=L // tm),
        grid=(T // tm, W // tc),
        in_specs=[
            pl.BlockSpec((tm, tc), lambda i, c: (i, c0 + c)),
            pl.BlockSpec((HALO_ROWS, tc), lambda i, c: (jnp.maximum(i * hb - 1, 0), c0 + c)),
            pl.BlockSpec((HALO_ROWS, tc), lambda i, c: (jnp.minimum((i + 1) * hb, n_halo - 1), c0 + c)),
            pl.BlockSpec((3, tc), lambda i, c: (0, c)),
            pl.BlockSpec((1, tc), lambda i, c: (0, c)),
        ],
        out_specs=pl.BlockSpec((tm, tc), lambda i, c: (i, c)),
        out_shape=jax.ShapeDtypeStruct((T, W), F32),
        compiler_params=_params("parallel", "parallel"), name="hyena_short_conv")(
            proj, proj, proj, w.astype(F32), b.astype(F32).reshape(1, W))


HY_CB = LANES
HY_NLO = 128
HY_SLAB_CHUNK = 16
HY_UNROLL = 4


def _dft_tables(L):
    N = 2 * L
    nhi = N // HY_NLO
    kb = jnp.arange(nhi, dtype=jnp.int32)
    ang = (2.0 * math.pi / nhi) * ((kb[:, None] * kb[None, :]) % nhi).astype(F32)
    f1 = jnp.concatenate([jnp.cos(ang), -jnp.sin(ang)], axis=0)
    ka = jnp.arange(HY_NLO, dtype=jnp.int32)
    idx = (ka[None, None, :] * (ka[None, :, None] * nhi + kb[:, None, None])) % N
    ang = (2.0 * math.pi / N) * idx.astype(F32)
    gr, gi = jnp.cos(ang), -jnp.sin(ang)
    g = jnp.concatenate([jnp.concatenate([gr, -gi], axis=2),
                         jnp.concatenate([gi, gr], axis=2)], axis=1)
    return f1.astype(BF16), g.astype(BF16)


def _dft_major_stage(src_ref, f1, a_ref, n_rows):
    two_nhi = f1.shape[0]

    def body(n_lo, carry):
        xs = src_ref[pl.ds(n_lo, n_rows, stride=HY_NLO), :]
        dst = pl.multiple_of(n_lo * two_nhi, two_nhi)
        a_ref[pl.ds(dst, two_nhi), :] = jnp.dot(f1, xs.astype(BF16), preferred_element_type=F32)
        return carry

    lax.fori_loop(0, HY_NLO, body, 0, unroll=HY_UNROLL)


def _load_slab(a_ref, kb, nhi):
    re = a_ref[pl.ds(kb, HY_NLO, stride=2 * nhi), :]
    im = a_ref[pl.ds(nhi + kb, HY_NLO, stride=2 * nhi), :]
    return jnp.concatenate([re, im], axis=0).astype(BF16)


def _hyena_spectrum_kernel(k_ref, f1_ref, g_ref, o_ref, a_ref, *, nhi):
    j = pl.program_id(1)

    @pl.when(j == 0)
    def _():
        _dft_major_stage(k_ref, f1_ref[...], a_ref, nhi)

    def slab(s, carry):
        kb = j * HY_SLAB_CHUNK + s
        o_ref[0, s] = jnp.dot(g_ref[s], _load_slab(a_ref, kb, nhi), preferred_element_type=F32)
        return carry

    lax.fori_loop(0, HY_SLAB_CHUNK, slab, 0, unroll=HY_UNROLL)


def _hyena_spectrum(k2, f1, g):
    N, n_ch = k2.shape
    nhi = N // HY_NLO
    return pl.pallas_call(
        functools.partial(_hyena_spectrum_kernel, nhi=nhi),
        grid=(n_ch // HY_CB, nhi // HY_SLAB_CHUNK),
        in_specs=[
            pl.BlockSpec((N, HY_CB), lambda c, j: (0, c)),
            pl.BlockSpec(f1.shape, lambda c, j: (0, 0)),
            pl.BlockSpec((HY_SLAB_CHUNK, 2 * HY_NLO, 2 * HY_NLO), lambda c, j: (j, 0, 0)),
        ],
        out_specs=pl.BlockSpec((1, HY_SLAB_CHUNK, 2 * HY_NLO, HY_CB), lambda c, j: (c, j, 0, 0)),
        out_shape=jax.ShapeDtypeStruct((n_ch // HY_CB, nhi, 2 * HY_NLO, HY_CB), F32),
        scratch_shapes=[pltpu.VMEM((HY_NLO * 2 * nhi, HY_CB), F32)],
        compiler_params=_params("parallel", "arbitrary"), name="hyena_spectrum")(k2, f1, g)


def _hyena_conv_kernel(u_ref, gate_ref, bias_ref, f1_ref, f1t_ref, g_ref, kf_ref, o_ref,
                       a_ref, y_ref, *, nhi):
    j = pl.program_id(2)
    n_in = nhi // 2
    L = n_in * HY_NLO

    @pl.when(j == 0)
    def _():
        _dft_major_stage(u_ref.at[0], f1_ref[...], a_ref, n_in)

    def forward(s, carry):
        kb = j * HY_SLAB_CHUNK + s
        t = jnp.dot(g_ref[s], _load_slab(a_ref, kb, nhi), preferred_element_type=F32)
        kf = kf_ref[0, s]
        tr, ti = t[:HY_NLO], t[HY_NLO:]
        kr, ki = kf[:HY_NLO], kf[HY_NLO:]
        y_ref[s] = jnp.concatenate([tr * kr - ti * ki, tr * ki + ti * kr], axis=0).astype(BF16)
        return carry

    lax.fori_loop(0, HY_SLAB_CHUNK, forward, 0, unroll=HY_UNROLL)

    def inverse(s, carry):
        kb = j * HY_SLAB_CHUNK + s
        r = lax.dot_general(g_ref[s], y_ref[s], (((0,), (0,)), ((), ())),
                            preferred_element_type=F32)
        a_ref[pl.ds(kb, HY_NLO, stride=2 * nhi), :] = r[:HY_NLO]
        a_ref[pl.ds(nhi + kb, HY_NLO, stride=2 * nhi), :] = r[HY_NLO:]
        return carry

    lax.fori_loop(0, HY_SLAB_CHUNK, inverse, 0, unroll=HY_UNROLL)

    @pl.when(j == pl.num_programs(2) - 1)
    def _():
        f1t = f1t_ref[...]

        def body(n_lo, carry):
            src = pl.multiple_of(n_lo * 2 * nhi, 2 * nhi)
            blk = a_ref[pl.ds(src, 2 * nhi), :].astype(BF16)
            o_ref[0, pl.ds(n_lo, n_in, stride=HY_NLO), :] = jnp.dot(
                f1t, blk, preferred_element_type=F32)
            return carry

        lax.fori_loop(0, HY_NLO, body, 0, unroll=HY_UNROLL)
        inv_n = 1.0 / (2 * L)
        rows = 512

        def gate_rows(c, carry):
            r0 = pl.multiple_of(c * rows, rows)
            sl = pl.ds(r0, rows)
            o_ref[0, sl, :] = gate_ref[0, sl, :] * (o_ref[0, sl, :] * inv_n
                                                    + u_ref[0, sl, :] * bias_ref[...])
            return carry

        lax.fori_loop(0, L // rows, gate_rows, 0)


def _hyena_conv(u3, u_col, gate3, gate_col, bias, kf, kf_row, f1, g):
    B, L, _ = u3.shape
    nhi = 2 * L // HY_NLO
    f1_in = f1[:, :nhi // 2]
    n_cb = HY_CH // HY_CB
    return pl.pallas_call(
        functools.partial(_hyena_conv_kernel, nhi=nhi),
        grid=(B, n_cb, nhi // HY_SLAB_CHUNK),
        in_specs=[
            pl.BlockSpec((1, L, HY_CB), lambda b, c, j: (b, 0, u_col + c)),
            pl.BlockSpec((1, L, HY_CB), lambda b, c, j: (b, 0, gate_col + c)),
            pl.BlockSpec((1, HY_CB), lambda b, c, j: (0, c)),
            pl.BlockSpec(f1_in.shape, lambda b, c, j: (0, 0)),
            pl.BlockSpec(f1_in.shape[::-1], lambda b, c, j: (0, 0)),
            pl.BlockSpec((HY_SLAB_CHUNK, 2 * HY_NLO, 2 * HY_NLO), lambda b, c, j: (j, 0, 0)),
            pl.BlockSpec((1, HY_SLAB_CHUNK, 2 * HY_NLO, HY_CB), lambda b, c, j: (kf_row + c, j, 0, 0)),
        ],
        out_specs=pl.BlockSpec((1, L, HY_CB), lambda b, c, j: (b, 0, c)),
        out_shape=jax.ShapeDtypeStruct((B, L, HY_CH), F32),
        scratch_shapes=[pltpu.VMEM((HY_NLO * 2 * nhi, HY_CB), F32),
                        pltpu.VMEM((HY_SLAB_CHUNK, 2 * HY_NLO, HY_CB), BF16)],
        compiler_params=_params("parallel", "parallel", "arbitrary"), name="hyena_long_conv")(
            u3, gate3, bias.astype(F32).reshape(1, HY_CH), f1_in, f1_in.T, g, kf)


def _hyena(proj, B, L, short_w, short_b, w1, b1, w2, b2, w3, freq, bias, f1, g):
    u = _short_conv(proj, L, short_w, short_b).reshape(B, L, (HY_ORDER + 1) * HY_CH)
    k = _hyena_filters(L, w1, b1, w2, b2, w3, freq).reshape(2 * L, HY_ORDER * HY_CH)
    kf = _hyena_spectrum(k, f1, g)
    n_cb = HY_CH // HY_CB
    z = _hyena_conv(u, 0, u, n_cb, bias[0], kf, 0, f1, g)
    return _hyena_conv(z, 0, u, 2 * n_cb, bias[1], kf, n_cb, f1, g)


def _mix_norm_kernel(na_ref, mla_ref, hy_ref, g_ref, o_ref):
    def rms(x, g):
        return (x * lax.rsqrt(jnp.mean(x * x, axis=-1, keepdims=True) + EPS) * g).astype(BF16)

    a, b = NA_WIDTH, NA_WIDTH + MLA_HEADS * MLA_V_DIM
    o_ref[:, :a] = rms(na_ref[...].astype(F32), g_ref[:, :a])
    o_ref[:, a:b] = rms(mla_ref[...].astype(F32), g_ref[:, a:b])
    o_ref[:, b:] = rms(hy_ref[...].astype(F32), g_ref[:, b:])


def _mix_norm(o_na, o_mla, o_hy, gain):
    T = o_na.shape[0]
    W = o_na.shape[1] + o_mla.shape[1] + o_hy.shape[1]
    tm = min(1024, T)
    row = lambda a: pl.BlockSpec((tm, a.shape[1]), lambda i: (i, 0))
    return pl.pallas_call(
        _mix_norm_kernel, grid=(T // tm,),
        in_specs=[row(o_na), row(o_mla), row(o_hy), pl.BlockSpec((1, W), lambda i: (0, 0))],
        out_specs=pl.BlockSpec((tm, W), lambda i: (i, 0)),
        out_shape=jax.ShapeDtypeStruct((T, W), BF16),
        compiler_params=_params("parallel"), name="mix_norm")(
            o_na, o_mla, o_hy, gain.astype(F32).reshape(1, W))


def _ffn_kernel(te_ref, na_ref, h_ref, wg_ref, wu_ref, wd_ref, rw_ref, o_ref, acc_ref):
    i = pl.program_id(0)
    j = pl.program_id(1)
    last = pl.num_programs(1) - 1
    active = i < na_ref[0]

    @pl.when(jnp.logical_and(active, j == 0))
    def _():
        acc_ref[...] = jnp.zeros_like(acc_ref)

    @pl.when(active)
    def _():
        h = h_ref[...]
        g = jnp.dot(h, wg_ref[0], preferred_element_type=F32)
        u = jnp.dot(h, wu_ref[0], preferred_element_type=F32)
        a = (g * (1.0 / (1.0 + jnp.exp(-g))) * u).astype(BF16)
        acc_ref[...] += jnp.dot(a, wd_ref[0], preferred_element_type=F32)

    @pl.when(jnp.logical_and(active, j == last))
    def _():
        o_ref[...] = (acc_ref[...] * rw_ref[...]).astype(o_ref.dtype)

    @pl.when(jnp.logical_and(jnp.logical_not(active), j == last))
    def _():
        o_ref[...] = jnp.zeros_like(o_ref)


def _ffn(h, w_gate, w_up, w_down, tile_expert, n_active, row_weight, tm):
    P, D = h.shape
    F = w_gate.shape[2]
    tf = 512
    nf = F // tf

    def fidx(i, j, na):
        return jnp.where(i < na[0], j, nf - 1)

    grid_spec = pltpu.PrefetchScalarGridSpec(
        num_scalar_prefetch=2, grid=(P // tm, nf),
        in_specs=[
            pl.BlockSpec((tm, D), lambda i, j, te, na: (i, 0)),
            pl.BlockSpec((1, D, tf), lambda i, j, te, na: (te[i], 0, fidx(i, j, na))),
            pl.BlockSpec((1, D, tf), lambda i, j, te, na: (te[i], 0, fidx(i, j, na))),
            pl.BlockSpec((1, tf, D), lambda i, j, te, na: (te[i], fidx(i, j, na), 0)),
            pl.BlockSpec((tm, 1), lambda i, j, te, na: (i, 0)),
        ],
        out_specs=pl.BlockSpec((tm, D), lambda i, j, te, na: (i, 0)),
        scratch_shapes=[pltpu.VMEM((tm, D), F32)])
    return pl.pallas_call(
        _ffn_kernel, grid_spec=grid_spec,
        out_shape=jax.ShapeDtypeStruct((P, D), BF16),
        compiler_params=_params("parallel", "arbitrary"), name="swiglu_ffn")(
            tile_expert, n_active, h, w_gate, w_up, w_down, row_weight)


def _dense_ffn(h, w_gate, w_up, w_down):
    T = h.shape[0]
    tm = min(1024, T)
    n = T // tm
    return _ffn(h, w_gate[None].astype(BF16), w_up[None].astype(BF16), w_down[None].astype(BF16),
                jnp.zeros((n,), jnp.int32), jnp.full((1,), n, jnp.int32),
                jnp.ones((T, 1), F32), tm)


def _moe_ffn(h, logits, w_gate, w_up, w_down):
    T, D = h.shape
    E = N_EXPERTS
    tm = min(1024, T)
    probs = jax.nn.softmax(logits[:, :E], axis=-1)
    top_p, top_i = lax.top_k(probs, TOP_K)
    top_p = top_p / jnp.sum(top_p, axis=-1, keepdims=True)
    flat_e = top_i.reshape(-1).astype(jnp.int32)
    n_slots = T * TOP_K
    order = jnp.argsort(flat_e, stable=True).astype(jnp.int32)
    counts = jnp.sum(flat_e[:, None] == jnp.arange(E, dtype=jnp.int32)[None, :], axis=0).astype(jnp.int32)
    tiles_per = (counts + tm - 1) // tm
    tile_end = jnp.cumsum(tiles_per)
    row_start = (tile_end - tiles_per) * tm
    slot_start = jnp.cumsum(counts) - counts
    n_tiles = n_slots // tm + E
    P = n_tiles * tm
    n_active = tile_end[-1:].astype(jnp.int32)
    tile_ids = jnp.arange(n_tiles, dtype=jnp.int32)
    tile_expert = jnp.sum(tile_ids[:, None] >= tile_end[None, :], axis=1).astype(jnp.int32)
    tile_expert = jnp.minimum(tile_expert, tile_expert[jnp.maximum(n_active[0] - 1, 0)])
    row_e = jnp.repeat(tile_expert, tm)
    rank = jnp.arange(P, dtype=jnp.int32) - row_start[row_e]
    row_ok = jnp.logical_and(jnp.repeat(tile_ids, tm) < n_active[0], rank < counts[row_e])
    row_slot = order[jnp.clip(slot_start[row_e] + rank, 0, n_slots - 1)]
    src_token = jnp.where(row_ok, row_slot // TOP_K, 0)
    row_weight = jnp.where(row_ok, top_p.reshape(-1)[row_slot], 0.0)
    sorted_pos = jnp.argsort(order).astype(jnp.int32)
    dest = row_start[flat_e] + sorted_pos - slot_start[flat_e]
    hs = jnp.take(h, src_token, axis=0)
    y = _ffn(hs, w_gate.astype(BF16), w_up.astype(BF16), w_down.astype(BF16),
             tile_expert, n_active, row_weight.reshape(P, 1), tm)
    dest = dest.reshape(T, TOP_K)
    return jnp.take(y, dest[:, 0], axis=0), jnp.take(y, dest[:, 1], axis=0)


def kernel(x, attn_norm_g, w_in, na_q_g, na_k_g, na_rpb, mla_q_norm_g, mla_kv_norm_g, mla_w_uq, mla_w_ukv, mla_q_g, mla_k_g, hy_short_w, hy_short_b, hy_w1, hy_b1, hy_w2, hy_b2, hy_w3, hy_freq, hy_bias, group_norm_g, w_out, ffn_norm_g, dense_w_gate, dense_w_up, dense_w_down, router_w, moe_w_gate, moe_w_up, moe_w_down):
    B, L, D = x.shape
    T = B * L
    depth = attn_norm_g.shape[0]
    assert L % (NA_K_ROWS * GRID_W) == 0 and w_in.shape[2] == IN_WIDTH
    x2 = x.reshape(T, D).astype(F32)
    f1, g_dft = _dft_tables(L)
    deltas = []
    for l in range(depth):
        x2, h = _add_norm(x2, deltas, attn_norm_g[l].astype(F32))
        w = w_in[l]
        w_in_p = jnp.concatenate(
            [w[:, :SRC_CKV], w[:, SRC_HY:], w[:, SRC_CKV:SRC_HY],
             jnp.zeros((D, IN_WIDTH_PAD - IN_WIDTH), w.dtype)], axis=1).astype(BF16)
        proj = _matmul(h, w_in_p, BF16, 1024)
        o_na = _neighborhood_attention(proj.reshape(B, L, IN_WIDTH_PAD), na_rpb[l],
                                       na_q_g[l], na_k_g[l]).reshape(T, NA_WIDTH)
        qn, qr, kn, kr, v = _mla_prep(proj, L, mla_q_norm_g[l], mla_kv_norm_g[l], mla_w_uq[l],
                                      mla_w_ukv[l], mla_q_g[l], mla_k_g[l])
        o_mla = _flash_attention(qn, qr, kn, kr, v, B, L).reshape(T, MLA_HEADS * MLA_V_DIM)
        o_hy = _hyena(proj, B, L, hy_short_w[l], hy_short_b[l], hy_w1[l], hy_b1[l], hy_w2[l],
                      hy_b2[l], hy_w3[l], hy_freq[l], hy_bias[l], f1, g_dft).reshape(T, HY_CH)
        mix = _mix_norm(o_na, o_mla, o_hy, group_norm_g[l])
        d_mix = _matmul(mix, w_out[l].astype(BF16), BF16, 1024)
        i = l // 2
        if l % 2 == 0:
            x2, h = _add_norm(x2, [d_mix], ffn_norm_g[l].astype(F32))
            deltas = [_dense_ffn(h, dense_w_gate[i], dense_w_up[i], dense_w_down[i])]
        else:
            wr = jnp.pad(router_w[i].astype(F32), ((0, 0), (0, LANES - N_EXPERTS)))
            x2, h, logits = _add_norm(x2, [d_mix], ffn_norm_g[l].astype(F32), wr)
            deltas = list(_moe_ffn(h, logits, moe_w_gate[i], moe_w_up[i], moe_w_down[i]))
    (x2,) = _add_norm(x2, deltas)
    return x2.reshape(B, L, D).astype(x.dtype)
```

```python
import functools
import math

import jax
import jax.numpy as jnp
import numpy as np
from jax import lax
from jax.experimental import pallas as pl
from jax.experimental.pallas import tpu as pltpu

F32 = jnp.float32
BF16 = jnp.bfloat16

GRID_W = 64
NA_HEADS = 8
NA_HEAD_DIM = 64
NA_WIDTH = NA_HEADS * NA_HEAD_DIM
NA_WIN_ROWS = 8
NA_WIN_COLS = 16
MLA_HEADS = 8
MLA_NOPE_DIM = 128
MLA_ROPE_DIM = 64
MLA_V_DIM = 128
MLA_QK_DIM = MLA_NOPE_DIM + MLA_ROPE_DIM
MLA_Q_RANK = 512
MLA_KV_RANK = 256
ROPE_THETA = 10000.0
HY_CH = 512
HY_ORDER = 2
HY_POS_BANDS = 16
HY_DECAY_TARGET = 1e-2
HY_DECAY_FAST = 0.3
HY_DECAY_SLOW = 1.5
HY_MAX_DECAY = math.log(HY_DECAY_TARGET) / HY_DECAY_FAST
HY_MIN_DECAY = math.log(HY_DECAY_TARGET) / HY_DECAY_SLOW
N_EXPERTS = 8
TOP_K = 2
EPS = 1e-6
NEG_INF = -1e30

SRC_KROPE = 3 * NA_WIDTH + MLA_Q_RANK + MLA_KV_RANK
SRC_HY = SRC_KROPE + MLA_ROPE_DIM
IN_WIDTH = SRC_HY + (HY_ORDER + 1) * HY_CH
SRC_CKV = 3 * NA_WIDTH + MLA_Q_RANK
COL_NA_Q = 0
COL_NA_K = NA_WIDTH
COL_NA_V = 2 * NA_WIDTH
COL_CQ = 3 * NA_WIDTH
COL_HY = COL_CQ + MLA_Q_RANK
COL_CKV = COL_HY + (HY_ORDER + 1) * HY_CH
COL_KROPE = COL_CKV + MLA_KV_RANK
LANES = 128
IN_WIDTH_PAD = 4096

VMEM_LIMIT_BYTES = 56 * 1024 * 1024

NA_Q_ROWS = 8
NA_K_ROWS = 16


def _params(*sem):
    return pltpu.CompilerParams(dimension_semantics=sem, vmem_limit_bytes=VMEM_LIMIT_BYTES)


def _add_norm_kernel(*refs, n_delta, with_router, with_norm):
    x_ref = refs[0]
    d_refs = refs[1:1 + n_delta]
    pos = 1 + n_delta
    x = x_ref[...]
    for d in d_refs:
        x = x + d[...].astype(F32)
    if not with_norm:
        refs[pos][...] = x
        return
    g_ref = refs[pos]
    pos += 1
    if with_router:
        wr_ref = refs[pos]
        pos += 1
    xo_ref, h_ref = refs[pos], refs[pos + 1]
    xo_ref[...] = x
    h = x * lax.rsqrt(jnp.mean(x * x, axis=-1, keepdims=True) + EPS) * g_ref[...]
    h_ref[...] = h.astype(BF16)
    if with_router:
        refs[pos + 2][...] = jnp.dot(h, wr_ref[...], preferred_element_type=F32,
                                     precision=lax.Precision.HIGHEST)


def _add_norm(x, deltas, gain=None, router_w=None):
    T, D = x.shape
    tm = min(512, T)
    with_norm = gain is not None
    with_router = router_w is not None
    row = pl.BlockSpec((tm, D), lambda i: (i, 0))
    in_specs = [row] + [row] * len(deltas)
    args = [x] + list(deltas)
    out_shape = [jax.ShapeDtypeStruct((T, D), F32)]
    out_specs = [row]
    if with_norm:
        in_specs.append(pl.BlockSpec((1, D), lambda i: (0, 0)))
        args.append(gain.reshape(1, D))
        if with_router:
            in_specs.append(pl.BlockSpec((D, LANES), lambda i: (0, 0)))
            args.append(router_w)
        out_shape.append(jax.ShapeDtypeStruct((T, D), BF16))
        out_specs.append(row)
        if with_router:
            out_shape.append(jax.ShapeDtypeStruct((T, LANES), F32))
            out_specs.append(pl.BlockSpec((tm, LANES), lambda i: (i, 0)))
    out = pl.pallas_call(
        functools.partial(_add_norm_kernel, n_delta=len(deltas), with_router=with_router,
                          with_norm=with_norm),
        grid=(T // tm,), in_specs=in_specs, out_specs=out_specs, out_shape=out_shape,
        compiler_params=_params("parallel"), name="add_norm")(*args)
    return out


def _mm_kernel(a_ref, b_ref, o_ref):
    o_ref[...] = jnp.dot(a_ref[...], b_ref[...], preferred_element_type=F32).astype(o_ref.dtype)


def _matmul(a, b, out_dtype, tn):
    M, K = a.shape
    N = b.shape[1]
    tm = min(1024, M)
    return pl.pallas_call(
        _mm_kernel, grid=(M // tm, N // tn),
        in_specs=[pl.BlockSpec((tm, K), lambda i, j: (i, 0)),
                  pl.BlockSpec((K, tn), lambda i, j: (0, j))],
        out_specs=pl.BlockSpec((tm, tn), lambda i, j: (i, j)),
        out_shape=jax.ShapeDtypeStruct((M, N), out_dtype),
        compiler_params=_params("parallel", "parallel"), name="matmul")(a, b)


def _na_bias_table(rpb, rows):
    n_groups = rows // NA_Q_ROWS
    reps = (0, min(1, n_groups - 1), n_groups - 1)
    row_off = np.zeros((3, NA_Q_ROWS, NA_K_ROWS), np.int32)
    row_ok = np.zeros((3, NA_Q_ROWS, NA_K_ROWS), bool)
    for p, g in enumerate(reps):
        start = int(np.clip(g * NA_Q_ROWS - NA_WIN_ROWS // 2, 0, rows - NA_K_ROWS))
        for i in range(NA_Q_ROWS):
            r = g * NA_Q_ROWS + i
            rs = int(np.clip(r - NA_WIN_ROWS // 2, 0, rows - NA_WIN_ROWS))
            for j in range(NA_K_ROWS):
                kr = start + j
                ok = rs <= kr < rs + NA_WIN_ROWS
                row_ok[p, i, j] = ok
                row_off[p, i, j] = np.clip(kr - r + NA_WIN_ROWS - 1, 0, 2 * NA_WIN_ROWS - 2)
    cols = np.arange(GRID_W)
    col_start = np.clip(cols - NA_WIN_COLS // 2, 0, GRID_W - NA_WIN_COLS)
    col_ok = (cols[None, :] >= col_start[:, None]) & (cols[None, :] < col_start[:, None] + NA_WIN_COLS)
    col_off = np.clip(cols[None, :] - cols[:, None] + NA_WIN_COLS - 1, 0, 2 * NA_WIN_COLS - 2)
    row_sel = np.eye(2 * NA_WIN_ROWS - 1, dtype=np.float32)[row_off]
    col_sel = np.eye(2 * NA_WIN_COLS - 1, dtype=np.float32)[col_off]
    tab = jnp.einsum('hrc,pijr,abc->hpiajb', rpb.astype(F32), row_sel, col_sel,
                     precision=lax.Precision.HIGHEST)
    ok = row_ok[:, :, None, :, None] & col_ok[None, None, :, None, :]
    tab = jnp.where(ok[None], tab, NEG_INF)
    return tab.reshape(NA_HEADS, 3, NA_Q_ROWS * GRID_W, NA_K_ROWS * GRID_W)


def _na_kernel(q_ref, k_ref, v_ref, bias_ref, gq_ref, gk_ref, ones_ref, o_ref, *, rows):
    g = pl.program_id(2)
    tq = NA_Q_ROWS * GRID_W
    tk = NA_K_ROWS * GRID_W
    start = jnp.clip(g * NA_Q_ROWS - NA_WIN_ROWS // 2, 0, rows - NA_K_ROWS) * GRID_W
    start = pl.multiple_of(start, GRID_W)
    ones_bd = ones_ref[...]

    def head_norm(x, gain):
        x2 = x * x
        hi = x2.astype(BF16)
        lo = (x2 - hi.astype(F32)).astype(BF16)
        ssq = (jnp.dot(hi, ones_bd, preferred_element_type=F32)
               + jnp.dot(lo, ones_bd, preferred_element_type=F32))
        return x * lax.rsqrt(ssq * (1.0 / NA_HEAD_DIM) + EPS) * gain

    q = head_norm(q_ref[0].astype(F32), gq_ref[...]) * (NA_HEAD_DIM ** -0.5)
    kw = head_norm(k_ref[0, pl.ds(start, tk), :].astype(F32), gk_ref[...]).astype(BF16)
    vw = v_ref[0, pl.ds(start, tk), :]
    lane = lax.broadcasted_iota(jnp.int32, (tq, LANES), 1)
    outs = []
    for hh in range(2):
        sel = (lane < NA_HEAD_DIM) if hh == 0 else (lane >= NA_HEAD_DIM)
        qm = jnp.where(sel, q, 0.0).astype(BF16)
        s = lax.dot_general(qm, kw, (((1,), (1,)), ((), ())), preferred_element_type=F32)
        s = s + bias_ref[hh, 0]
        m = jnp.max(s, axis=-1, keepdims=True)
        p = jnp.exp(s - m)
        l = jnp.sum(p, axis=-1, keepdims=True)
        o = jnp.dot(p.astype(BF16), vw, preferred_element_type=F32)
        outs.append(o / l)
    o_ref[0] = jnp.where(lane < NA_HEAD_DIM, outs[0], outs[1]).astype(o_ref.dtype)


def _neighborhood_attention(proj3, rpb, q_g, k_g):
    B, L, _ = proj3.shape
    rows = L // GRID_W
    n_groups = rows // NA_Q_ROWS
    tq = NA_Q_ROWS * GRID_W
    tk = NA_K_ROWS * GRID_W
    bias = _na_bias_table(rpb, rows)
    gq = jnp.tile(q_g.astype(F32), 2).reshape(1, LANES)
    gk = jnp.tile(k_g.astype(F32), 2).reshape(1, LANES)
    ones_bd = jnp.asarray(np.kron(np.eye(2), np.ones((NA_HEAD_DIM, NA_HEAD_DIM))), BF16)
    qb, kb, vb = COL_NA_Q // LANES, COL_NA_K // LANES, COL_NA_V // LANES

    def pattern(g):
        return (g > 0).astype(jnp.int32) + (g == n_groups - 1).astype(jnp.int32)

    return pl.pallas_call(
        functools.partial(_na_kernel, rows=rows),
        grid=(NA_HEADS // 2, B, n_groups),
        in_specs=[
            pl.BlockSpec((1, tq, LANES), lambda hp, b, g: (b, g, qb + hp)),
            pl.BlockSpec((1, L, LANES), lambda hp, b, g: (b, 0, kb + hp)),
            pl.BlockSpec((1, L, LANES), lambda hp, b, g: (b, 0, vb + hp)),
            pl.BlockSpec((2, 1, tq, tk), lambda hp, b, g: (hp, pattern(g), 0, 0)),
            pl.BlockSpec((1, LANES), lambda hp, b, g: (0, 0)),
            pl.BlockSpec((1, LANES), lambda hp, b, g: (0, 0)),
            pl.BlockSpec((LANES, LANES), lambda hp, b, g: (0, 0)),
        ],
        out_specs=pl.BlockSpec((1, tq, LANES), lambda hp, b, g: (b, g, hp)),
        out_shape=jax.ShapeDtypeStruct((B, L, NA_WIDTH), BF16),
        compiler_params=_params("parallel", "parallel", "arbitrary"),
        name="neighborhood_attention")(proj3, proj3, proj3, bias, gq, gk, ones_bd)


def _mla_prep_kernel(cq_ref, ckv_ref, krope_ref, gqn_ref, gkvn_ref, wuq_ref, wukv_ref,
                     qgn_ref, qgr_ref, kgn_ref, kgr_ref, cos_ref, sin_ref,
                     qn_ref, qr_ref, kn_ref, kr_ref, v_ref):
    def rms(x, g):
        return x * lax.rsqrt(jnp.mean(x * x, axis=-1, keepdims=True) + EPS) * g

    tm = cq_ref.shape[0]
    nope_w = MLA_HEADS * MLA_NOPE_DIM
    cq = rms(cq_ref[...].astype(F32), gqn_ref[...]).astype(BF16)
    q = jnp.dot(cq, wuq_ref[...], preferred_element_type=F32)
    ckv = rms(ckv_ref[...].astype(F32), gkvn_ref[...]).astype(BF16)
    kv = jnp.dot(ckv, wukv_ref[...], preferred_element_type=F32)
    v_ref[...] = kv[:, nope_w:].astype(BF16)

    lane = lax.broadcasted_iota(jnp.int32, (tm, LANES), 1)
    first = lane < MLA_ROPE_DIM
    low_half = (lane % MLA_ROPE_DIM) < (MLA_ROPE_DIM // 2)
    cosv = cos_ref[...]
    sinv = sin_ref[...]
    scale = (MLA_QK_DIM ** -0.5) * math.log2(math.e)
    inv_dim = 1.0 / MLA_QK_DIM

    def rope(x):
        swapped = jnp.where(low_half, pltpu.roll(x, LANES - MLA_ROPE_DIM // 2, 1),
                            pltpu.roll(x, MLA_ROPE_DIM // 2, 1))
        return x * cosv + swapped * sinv

    qgn, qgr = qgn_ref[...], qgr_ref[...]
    kgn, kgr = kgn_ref[...], kgr_ref[...]
    kraw = krope_ref[...].astype(F32)
    kdup = jnp.where(first, kraw, pltpu.roll(kraw, MLA_ROPE_DIM, 1))
    k_rope_sq = jnp.where(first, kraw * kraw, 0.0)
    k_roped = rope(kdup * kgr)

    for j in range(MLA_HEADS // 2):
        qr_j = q[:, nope_w + LANES * j:nope_w + LANES * (j + 1)]
        qr_sq = qr_j * qr_j
        q_rs, k_rs = [], []
        for hh in range(2):
            h = 2 * j + hh
            cols = slice(LANES * h, LANES * (h + 1))
            qn_h = q[:, cols]
            t = qn_h * qn_h + jnp.where(first if hh == 0 else jnp.logical_not(first), qr_sq, 0.0)
            rs = lax.rsqrt(jnp.sum(t, axis=-1, keepdims=True) * inv_dim + EPS)
            qn_ref[:, cols] = (qn_h * rs * qgn * scale).astype(BF16)
            q_rs.append(rs)
            kn_h = kv[:, cols]
            t = kn_h * kn_h + k_rope_sq
            rs = lax.rsqrt(jnp.sum(t, axis=-1, keepdims=True) * inv_dim + EPS)
            kn_ref[:, cols] = (kn_h * rs * kgn).astype(BF16)
            k_rs.append(rs)
        pair = slice(LANES * j, LANES * (j + 1))
        qr_ref[:, pair] = (rope(qr_j * jnp.where(first, q_rs[0], q_rs[1]) * qgr) * scale).astype(BF16)
        kr_ref[:, pair] = (k_roped * jnp.where(first, k_rs[0], k_rs[1])).astype(BF16)


def _mla_prep(proj, L, q_norm_g, kv_norm_g, w_uq, w_ukv, q_head_g, k_head_g):
    T = proj.shape[0]
    tm = min(512, L)
    H = MLA_HEADS
    wq = w_uq.reshape(MLA_Q_RANK, H, MLA_QK_DIM)
    wq = jnp.concatenate([wq[:, :, :MLA_NOPE_DIM].reshape(MLA_Q_RANK, -1),
                          wq[:, :, MLA_NOPE_DIM:].reshape(MLA_Q_RANK, -1)], axis=1).astype(BF16)
    wkv = w_ukv.reshape(MLA_KV_RANK, H, MLA_NOPE_DIM + MLA_V_DIM)
    wkv = jnp.concatenate([wkv[:, :, :MLA_NOPE_DIM].reshape(MLA_KV_RANK, -1),
                           wkv[:, :, MLA_NOPE_DIM:].reshape(MLA_KV_RANK, -1)], axis=1).astype(BF16)
    half = MLA_ROPE_DIM // 2
    freqs = ROPE_THETA ** (-jnp.arange(half, dtype=F32) / half)
    ang = jnp.arange(L).astype(F32)[:, None] * freqs[None, :]
    cos, sin = jnp.cos(ang), jnp.sin(ang)
    cos2 = jnp.tile(jnp.concatenate([cos, cos], axis=1), (1, 2))
    sin2 = jnp.tile(jnp.concatenate([-sin, sin], axis=1), (1, 2))

    def vec(g, reps=1):
        return jnp.tile(g.astype(F32), reps).reshape(1, -1)

    const = lambda shape: pl.BlockSpec(shape, lambda i: (0, 0))
    nblk = L // tm
    outs = pl.pallas_call(
        _mla_prep_kernel, grid=(T // tm,),
        in_specs=[
            pl.BlockSpec((tm, MLA_Q_RANK), lambda i: (i, COL_CQ // MLA_Q_RANK)),
            pl.BlockSpec((tm, MLA_KV_RANK), lambda i: (i, COL_CKV // MLA_KV_RANK)),
            pl.BlockSpec((tm, LANES), lambda i: (i, COL_KROPE // LANES)),
            const((1, MLA_Q_RANK)), const((1, MLA_KV_RANK)),
            const(wq.shape), const(wkv.shape),
            const((1, LANES)), const((1, LANES)), const((1, LANES)), const((1, LANES)),
            pl.BlockSpec((tm, LANES), lambda i: (i % nblk, 0)),
            pl.BlockSpec((tm, LANES), lambda i: (i % nblk, 0)),
        ],
        out_specs=[
            pl.BlockSpec((tm, H * MLA_NOPE_DIM), lambda i: (i, 0)),
            pl.BlockSpec((tm, H * MLA_ROPE_DIM), lambda i: (i, 0)),
            pl.BlockSpec((tm, H * MLA_NOPE_DIM), lambda i: (i, 0)),
            pl.BlockSpec((tm, H * MLA_ROPE_DIM), lambda i: (i, 0)),
            pl.BlockSpec((tm, H * MLA_V_DIM), lambda i: (i, 0)),
        ],
        out_shape=[
            jax.ShapeDtypeStruct((T, H * MLA_NOPE_DIM), BF16),
            jax.ShapeDtypeStruct((T, H * MLA_ROPE_DIM), BF16),
            jax.ShapeDtypeStruct((T, H * MLA_NOPE_DIM), BF16),
            jax.ShapeDtypeStruct((T, H * MLA_ROPE_DIM), BF16),
            jax.ShapeDtypeStruct((T, H * MLA_V_DIM), BF16),
        ],
        compiler_params=_params("parallel"), name="mla_prep")(
            proj, proj, proj, vec(q_norm_g), vec(kv_norm_g), wq, wkv,
            vec(q_head_g[:MLA_NOPE_DIM]), vec(q_head_g[MLA_NOPE_DIM:], 2),
            vec(k_head_g[:MLA_NOPE_DIM]), vec(k_head_g[MLA_NOPE_DIM:], 2), cos2, sin2)
    return outs


FLASH_STREAM_ROWS = 256


def _flash_kernel(qn_ref, qr_ref, kn0_ref, kr0_ref, kn1_ref, kr1_ref, v_ref, o_ref,
                  q_sc, s_even, s_odd, m_sc, acc_sc):
    h = pl.program_id(1)
    ki = pl.program_id(3)
    tq = qn_ref.shape[1]
    tk = kn0_ref.shape[1]
    rows = min(FLASH_STREAM_ROWS, tq)
    n_streams = tq // rows
    nt = (((1,), (1,)), ((), ()))

    @pl.when(ki == 0)
    def _():
        lane = lax.broadcasted_iota(jnp.int32, (tq, LANES), 1)
        own = (lane // MLA_ROPE_DIM) == (h % 2)
        q_sc[:, :LANES] = qn_ref[0]
        q_sc[:, LANES:] = jnp.where(own, qr_ref[0], jnp.zeros_like(qr_ref[0]))
        m_sc[...] = jnp.full_like(m_sc, NEG_INF)
        acc_sc[...] = jnp.zeros_like(acc_sc)
        k0 = jnp.concatenate([kn0_ref[0], kr0_ref[0]], axis=-1)
        for r in range(n_streams):
            sl = slice(r * rows, (r + 1) * rows)
            s_even[sl, :] = lax.dot_general(q_sc[sl, :], k0, nt, preferred_element_type=F32)

    def step(s_cur, s_next):
        k1 = jnp.concatenate([kn1_ref[0], kr1_ref[0]], axis=-1)
        v_ones = jnp.concatenate([v_ref[0], jnp.ones((tk, LANES), BF16)], axis=-1)
        for r in range(n_streams):
            sl = slice(r * rows, (r + 1) * rows)
            s_next[sl, :] = lax.dot_general(q_sc[sl, :], k1, nt, preferred_element_type=F32)
            s = s_cur[sl, :]
            m_prev = m_sc[sl, :]
            m_new = jnp.maximum(m_prev, jnp.max(s, axis=-1, keepdims=True))
            p = jnp.exp2(s - m_new).astype(BF16)
            acc_sc[sl, :] = (jnp.exp2(m_prev - m_new) * acc_sc[sl, :]
                             + jnp.dot(p, v_ones, preferred_element_type=F32))
            m_sc[sl, :] = m_new

    @pl.when(ki % 2 == 0)
    def _():
        step(s_even, s_odd)

    @pl.when(ki % 2 == 1)
    def _():
        step(s_odd, s_even)

    @pl.when(ki == pl.num_programs(3) - 1)
    def _():
        o_ref[0] = (acc_sc[:, :LANES] / acc_sc[:, LANES:]).astype(o_ref.dtype)


def _flash_attention(qn, qr, kn, kr, v, B, L):
    H = MLA_HEADS
    tq = min(2048, L)
    tk = min(1024, L)
    r3 = lambda a: a.reshape(B, L, a.shape[-1])
    nk = L // tk
    ahead = lambda ki: jnp.minimum(ki + 1, nk - 1)
    return pl.pallas_call(
        _flash_kernel, grid=(B, H, L // tq, nk),
        in_specs=[
            pl.BlockSpec((1, tq, LANES), lambda b, h, qi, ki: (b, qi, h)),
            pl.BlockSpec((1, tq, LANES), lambda b, h, qi, ki: (b, qi, h // 2)),
            pl.BlockSpec((1, tk, LANES), lambda b, h, qi, ki: (b, 0, h)),
            pl.BlockSpec((1, tk, LANES), lambda b, h, qi, ki: (b, 0, h // 2)),
            pl.BlockSpec((1, tk, LANES), lambda b, h, qi, ki: (b, ahead(ki), h)),
            pl.BlockSpec((1, tk, LANES), lambda b, h, qi, ki: (b, ahead(ki), h // 2)),
            pl.BlockSpec((1, tk, LANES), lambda b, h, qi, ki: (b, ki, h)),
        ],
        out_specs=pl.BlockSpec((1, tq, LANES), lambda b, h, qi, ki: (b, qi, h)),
        out_shape=jax.ShapeDtypeStruct((B, L, H * MLA_V_DIM), BF16),
        scratch_shapes=[pltpu.VMEM((tq, 2 * LANES), BF16), pltpu.VMEM((tq, tk), F32),
                        pltpu.VMEM((tq, tk), F32), pltpu.VMEM((tq, 1), F32),
                        pltpu.VMEM((tq, 2 * LANES), F32)],
        compiler_params=_params("parallel", "parallel", "parallel", "arbitrary"),
        name="mla_flash")(r3(qn), r3(qr), r3(kn), r3(kr), r3(kn), r3(kr), r3(v))


def _hyena_filters(L, w1, b1, w2, b2, w3, freq):
    t = jnp.linspace(0.0, 1.0, L, dtype=F32)[:, None]
    bands = jnp.linspace(1e-4, HY_POS_BANDS - 1, HY_POS_BANDS, dtype=F32)[None, :]
    w = 2.0 * math.pi * jnp.arange(L, dtype=F32)[:, None] / L
    z = jnp.concatenate([t, jnp.cos(bands * w), -jnp.sin(bands * w)], axis=-1)
    hp = lax.Precision.HIGHEST
    deltas = jnp.abs(jnp.linspace(HY_MIN_DECAY, HY_MAX_DECAY, HY_CH, dtype=F32))
    w3d = w3.astype(F32).reshape(w3.shape[0], 2, HY_ORDER * HY_CH)

    def direction(z, t, d):
        a = jnp.sin(freq[0] * (jnp.matmul(z, w1, precision=hp) + b1))
        a = jnp.sin(freq[1] * (jnp.matmul(a, w2, precision=hp) + b2))
        f = jnp.matmul(a, w3d[:, d], precision=hp).reshape(L, HY_ORDER, HY_CH)
        return f * jnp.exp(-t * deltas[None, :])[:, None, :]

    fwd = direction(z, t, 0)
    bwd_rev = direction(z[::-1], t[::-1], 1)[:L - 1]
    k = jnp.concatenate([fwd, jnp.zeros((1, HY_ORDER, HY_CH), F32), bwd_rev], axis=0)
    return k * lax.rsqrt(jnp.sum(k * k, axis=0, keepdims=True) + EPS)


HALO_ROWS = 16


def _short_conv_kernel(x_ref, xp_ref, xn_ref, w_ref, b_ref, o_ref, *, tiles_per_seq):
    i = pl.program_id(0)
    x = x_ref[...].astype(F32)
    tm = x.shape[0]
    row = lax.broadcasted_iota(jnp.int32, x.shape, 0)
    t = i % tiles_per_seq
    halo_prev = jnp.where(t == 0, 0.0, xp_ref[HALO_ROWS - 1:HALO_ROWS, :].astype(F32))
    halo_next = jnp.where(t == tiles_per_seq - 1, 0.0, xn_ref[0:1, :].astype(F32))
    prev = jnp.where(row == 0, halo_prev, pltpu.roll(x, 1, 0))
    nxt = jnp.where(row == tm - 1, halo_next, pltpu.roll(x, tm - 1, 0))
    o_ref[...] = prev * w_ref[0:1, :] + x * w_ref[1:2, :] + nxt * w_ref[2:3, :] + b_ref[...]


def _short_conv(proj, L, w, b):
    T = proj.shape[0]
    W = (HY_ORDER + 1) * HY_CH
    tm = min(1024, L)
    tc = 512
    c0 = COL_HY // tc
    hb = tm // HALO_ROWS
    n_halo = T // HALO_ROWS
    return pl.pallas_call(
        functools.partial(_short_conv_kernel, tiles_per_seq=L // tm),
        grid=(T // tm, W // tc),
        in_specs=[
            pl.BlockSpec((tm, tc), lambda i, c: (i, c0 + c)),
            pl.BlockSpec((HALO_ROWS, tc), lambda i, c: (jnp.maximum(i * hb - 1, 0), c0 + c)),
            pl.BlockSpec((HALO_ROWS, tc), lambda i, c: (jnp.minimum((i + 1) * hb, n_halo - 1), c0 + c)),
            pl.BlockSpec((3, tc), lambda i, c: (0, c)),
            pl.BlockSpec((1, tc), lambda i, c: (0, c)),
        ],
        out_specs=pl.BlockSpec((tm, tc), lambda i, c: (i, c)),
        out_shape=jax.ShapeDtypeStruct((T, W), F32),
        compiler_params=_params("parallel", "parallel"), name="hyena_short_conv")(
            proj, proj, proj, w.astype(F32), b.astype(F32).reshape(1, W))


HY_CB = LANES
HY_NLO = 128
HY_SLAB_CHUNK = 16
HY_UNROLL = 4


def _dft_tables(L):
    N = 2 * L
    nhi = N // HY_NLO
    kb = jnp.arange(nhi, dtype=jnp.int32)
    ang = (2.0 * math.pi / nhi) * ((kb[:, None] * kb[None, :]) % nhi).astype(F32)
    f1 = jnp.concatenate([jnp.cos(ang), -jnp.sin(ang)], axis=0)
    ka = jnp.arange(HY_NLO, dtype=jnp.int32)
    idx = (ka[None, None, :] * (ka[None, :, None] * nhi + kb[:, None, None])) % N
    ang = (2.0 * math.pi / N) * idx.astype(F32)
    gr, gi = jnp.cos(ang), -jnp.sin(ang)
    g = jnp.concatenate([jnp.concatenate([gr, -gi], axis=2),
                         jnp.concatenate([gi, gr], axis=2)], axis=1)
    return f1.astype(BF16), g.astype(BF16)


def _dft_major_stage(src_ref, f1, a_ref, n_rows):
    two_nhi = f1.shape[0]

    def body(n_lo, carry):
        xs = src_ref[pl.ds(n_lo, n_rows, stride=HY_NLO), :]
        dst = pl.multiple_of(n_lo * two_nhi, two_nhi)
        a_ref[pl.ds(dst, two_nhi), :] = jnp.dot(f1, xs.astype(BF16), preferred_element_type=F32)
        return carry

    lax.fori_loop(0, HY_NLO, body, 0, unroll=HY_UNROLL)


def _load_slab(a_ref, kb, nhi):
    re = a_ref[pl.ds(kb, HY_NLO, stride=2 * nhi), :]
    im = a_ref[pl.ds(nhi + kb, HY_NLO, stride=2 * nhi), :]
    return jnp.concatenate([re, im], axis=0).astype(BF16)


def _hyena_spectrum_kernel(k_ref, f1_ref, g_ref, o_ref, a_ref, *, nhi):
    j = pl.program_id(1)

    @pl.when(j == 0)
    def _():
        _dft_major_stage(k_ref, f1_ref[...], a_ref, nhi)

    def slab(s, carry):
        kb = j * HY_SLAB_CHUNK + s
        o_ref[0, s] = jnp.dot(g_ref[s], _load_slab(a_ref, kb, nhi), preferred_element_type=F32)
        return carry

    lax.fori_loop(0, HY_SLAB_CHUNK, slab, 0, unroll=HY_UNROLL)


def _hyena_spectrum(k2, f1, g):
    N, n_ch = k2.shape
    nhi = N // HY_NLO
    return pl.pallas_call(
        functools.partial(_hyena_spectrum_kernel, nhi=nhi),
        grid=(n_ch // HY_CB, nhi // HY_SLAB_CHUNK),
        in_specs=[
            pl.BlockSpec((N, HY_CB), lambda c, j: (0, c)),
            pl.BlockSpec(f1.shape, lambda c, j: (0, 0)),
            pl.BlockSpec((HY_SLAB_CHUNK, 2 * HY_NLO, 2 * HY_NLO), lambda c, j: (j, 0, 0)),
        ],
        out_specs=pl.BlockSpec((1, HY_SLAB_CHUNK, 2 * HY_NLO, HY_CB), lambda c, j: (c, j, 0, 0)),
        out_shape=jax.ShapeDtypeStruct((n_ch // HY_CB, nhi, 2 * HY_NLO, HY_CB), F32),
        scratch_shapes=[pltpu.VMEM((HY_NLO * 2 * nhi, HY_CB), F32)],
        compiler_params=_params("parallel", "arbitrary"), name="hyena_spectrum")(k2, f1, g)


def _hyena_conv_kernel(u_ref, gate_ref, bias_ref, f1_ref, f1t_ref, g_ref, kf_ref, o_ref,
                       a_ref, y_ref, *, nhi):
    j = pl.program_id(2)
    n_in = nhi // 2
    L = n_in * HY_NLO

    @pl.when(j == 0)
    def _():
        _dft_major_stage(u_ref.at[0], f1_ref[...], a_ref, n_in)

    def forward(s, carry):
        kb = j * HY_SLAB_CHUNK + s
        t = jnp.dot(g_ref[s], _load_slab(a_ref, kb, nhi), preferred_element_type=F32)
        kf = kf_ref[0, s]
        tr, ti = t[:HY_NLO], t[HY_NLO:]
        kr, ki = kf[:HY_NLO], kf[HY_NLO:]
        y_ref[s] = jnp.concatenate([tr * kr - ti * ki, tr * ki + ti * kr], axis=0).astype(BF16)
        return carry

    lax.fori_loop(0, HY_SLAB_CHUNK, forward, 0, unroll=HY_UNROLL)

    def inverse(s, carry):
        kb = j * HY_SLAB_CHUNK + s
        r = lax.dot_general(g_ref[s], y_ref[s], (((0,), (0,)), ((), ())),
                            preferred_element_type=F32)
        a_ref[pl.ds(kb, HY_NLO, stride=2 * nhi), :] = r[:HY_NLO]
        a_ref[pl.ds(nhi + kb, HY_NLO, stride=2 * nhi), :] = r[HY_NLO:]
        return carry

    lax.fori_loop(0, HY_SLAB_CHUNK, inverse, 0, unroll=HY_UNROLL)

    @pl.when(j == pl.num_programs(2) - 1)
    def _():
        f1t = f1t_ref[...]

        def body(n_lo, carry):
            src = pl.multiple_of(n_lo * 2 * nhi, 2 * nhi)
            blk = a_ref[pl.ds(src, 2 * nhi), :].astype(BF16)
            o_ref[0, pl.ds(n_lo, n_in, stride=HY_NLO), :] = jnp.dot(
                f1t, blk, preferred_element_type=F32)
            return carry

        lax.fori_loop(0, HY_NLO, body, 0, unroll=HY_UNROLL)
        inv_n = 1.0 / (2 * L)
        rows = 512

        def gate_rows(c, carry):
            r0 = pl.multiple_of(c * rows, rows)
            sl = pl.ds(r0, rows)
            o_ref[0, sl, :] = gate_ref[0, sl, :] * (o_ref[0, sl, :] * inv_n
                                                    + u_ref[0, sl, :] * bias_ref[...])
            return carry

        lax.fori_loop(0, L // rows, gate_rows, 0)


def _hyena_conv(u3, u_col, gate3, gate_col, bias, kf, kf_row, f1, g):
    B, L, _ = u3.shape
    nhi = 2 * L // HY_NLO
    f1_in = f1[:, :nhi // 2]
    n_cb = HY_CH // HY_CB
    return pl.pallas_call(
        functools.partial(_hyena_conv_kernel, nhi=nhi),
        grid=(B, n_cb, nhi // HY_SLAB_CHUNK),
        in_specs=[
            pl.BlockSpec((1, L, HY_CB), lambda b, c, j: (b, 0, u_col + c)),
            pl.BlockSpec((1, L, HY_CB), lambda b, c, j: (b, 0, gate_col + c)),
            pl.BlockSpec((1, HY_CB), lambda b, c, j: (0, c)),
            pl.BlockSpec(f1_in.shape, lambda b, c, j: (0, 0)),
            pl.BlockSpec(f1_in.shape[::-1], lambda b, c, j: (0, 0)),
            pl.BlockSpec((HY_SLAB_CHUNK, 2 * HY_NLO, 2 * HY_NLO), lambda b, c, j: (j, 0, 0)),
            pl.BlockSpec((1, HY_SLAB_CHUNK, 2 * HY_NLO, HY_CB), lambda b, c, j: (kf_row + c, j, 0, 0)),
        ],
        out_specs=pl.BlockSpec((1, L, HY_CB), lambda b, c, j: (b, 0, c)),
        out_shape=jax.ShapeDtypeStruct((B, L, HY_CH), F32),
        scratch_shapes=[pltpu.VMEM((HY_NLO * 2 * nhi, HY_CB), F32),
                        pltpu.VMEM((HY_SLAB_CHUNK, 2 * HY_NLO, HY_CB), BF16)],
        compiler_params=_params("parallel", "parallel", "arbitrary"), name="hyena_long_conv")(
            u3, gate3, bias.astype(F32).reshape(1, HY_CH), f1_in, f1_in.T, g, kf)


def _hyena(proj, B, L, short_w, short_b, w1, b1, w2, b2, w3, freq, bias, f1, g):
    u = _short_conv(proj, L, short_w, short_b).reshape(B, L, (HY_ORDER + 1) * HY_CH)
    k = _hyena_filters(L, w1, b1, w2, b2, w3, freq).reshape(2 * L, HY_ORDER * HY_CH)
    kf = _hyena_spectrum(k, f1, g)
    n_cb = HY_CH // HY_CB
    z = _hyena_conv(u, 0, u, n_cb, bias[0], kf, 0, f1, g)
    return _hyena_conv(z, 0, u, 2 * n_cb, bias[1], kf, n_cb, f1, g)


def _mix_norm_kernel(na_ref, mla_ref, hy_ref, g_ref, o_ref):
    def rms(x, g):
        return (x * lax.rsqrt(jnp.mean(x * x, axis=-1, keepdims=True) + EPS) * g).astype(BF16)

    a, b = NA_WIDTH, NA_WIDTH + MLA_HEADS * MLA_V_DIM
    o_ref[:, :a] = rms(na_ref[...].astype(F32), g_ref[:, :a])
    o_ref[:, a:b] = rms(mla_ref[...].astype(F32), g_ref[:, a:b])
    o_ref[:, b:] = rms(hy_ref[...].astype(F32), g_ref[:, b:])


def _mix_norm(o_na, o_mla, o_hy, gain):
    T = o_na.shape[0]
    W = o_na.shape[1] + o_mla.shape[1] + o_hy.shape[1]
    tm = min(1024, T)
    row = lambda a: pl.BlockSpec((tm, a.shape[1]), lambda i: (i, 0))
    return pl.pallas_call(
        _mix_norm_kernel, grid=(T // tm,),
        in_specs=[row(o_na), row(o_mla), row(o_hy), pl.BlockSpec((1, W), lambda i: (0, 0))],
        out_specs=pl.BlockSpec((tm, W), lambda i: (i, 0)),
        out_shape=jax.ShapeDtypeStruct((T, W), BF16),
        compiler_params=_params("parallel"), name="mix_norm")(
            o_na, o_mla, o_hy, gain.astype(F32).reshape(1, W))


def _ffn_kernel(te_ref, na_ref, h_ref, wg_ref, wu_ref, wd_ref, rw_ref, o_ref, acc_ref):
    i = pl.program_id(0)
    j = pl.program_id(1)
    last = pl.num_programs(1) - 1
    active = i < na_ref[0]

    @pl.when(jnp.logical_and(active, j == 0))
    def _():
        acc_ref[...] = jnp.zeros_like(acc_ref)

    @pl.when(active)
    def _():
        h = h_ref[...]
        g = jnp.dot(h, wg_ref[0], preferred_element_type=F32)
        u = jnp.dot(h, wu_ref[0], preferred_element_type=F32)
        a = (g * (1.0 / (1.0 + jnp.exp(-g))) * u).astype(BF16)
        acc_ref[...] += jnp.dot(a, wd_ref[0], preferred_element_type=F32)

    @pl.when(jnp.logical_and(active, j == last))
    def _():
        o_ref[...] = (acc_ref[...] * rw_ref[...]).astype(o_ref.dtype)

    @pl.when(jnp.logical_and(jnp.logical_not(active), j == last))
    def _():
        o_ref[...] = jnp.zeros_like(o_ref)


def _ffn(h, w_gate, w_up, w_down, tile_expert, n_active, row_weight, tm):
    P, D = h.shape
    F = w_gate.shape[2]
    tf = 512
    nf = F // tf

    def fidx(i, j, na):
        return jnp.where(i < na[0], j, nf - 1)

    grid_spec = pltpu.PrefetchScalarGridSpec(
        num_scalar_prefetch=2, grid=(P // tm, nf),
        in_specs=[
            pl.BlockSpec((tm, D), lambda i, j, te, na: (i, 0)),
            pl.BlockSpec((1, D, tf), lambda i, j, te, na: (te[i], 0, fidx(i, j, na))),
            pl.BlockSpec((1, D, tf), lambda i, j, te, na: (te[i], 0, fidx(i, j, na))),
            pl.BlockSpec((1, tf, D), lambda i, j, te, na: (te[i], fidx(i, j, na), 0)),
            pl.BlockSpec((tm, 1), lambda i, j, te, na: (i, 0)),
        ],
        out_specs=pl.BlockSpec((tm, D), lambda i, j, te, na: (i, 0)),
        scratch_shapes=[pltpu.VMEM((tm, D), F32)])
    return pl.pallas_call(
        _ffn_kernel, grid_spec=grid_spec,
        out_shape=jax.ShapeDtypeStruct((P, D), BF16),
        compiler_params=_params("parallel", "arbitrary"), name="swiglu_ffn")(
            tile_expert, n_active, h, w_gate, w_up, w_down, row_weight)


def _dense_ffn(h, w_gate, w_up, w_down):
    T = h.shape[0]
    tm = min(1024, T)
    n = T // tm
    return _ffn(h, w_gate[None].astype(BF16), w_up[None].astype(BF16), w_down[None].astype(BF16),
                jnp.zeros((n,), jnp.int32), jnp.full((1,), n, jnp.int32),
                jnp.ones((T, 1), F32), tm)


def _moe_ffn(h, logits, w_gate, w_up, w_down):
    T, D = h.shape
    E = N_EXPERTS
    tm = min(1024, T)
    probs = jax.nn.softmax(logits[:, :E], axis=-1)
    top_p, top_i = lax.top_k(probs, TOP_K)
    top_p = top_p / jnp.sum(top_p, axis=-1, keepdims=True)
    flat_e = top_i.reshape(-1).astype(jnp.int32)
    n_slots = T * TOP_K
    order = jnp.argsort(flat_e, stable=True).astype(jnp.int32)
    counts = jnp.sum(flat_e[:, None] == jnp.arange(E, dtype=jnp.int32)[None, :], axis=0).astype(jnp.int32)
    tiles_per = (counts + tm - 1) // tm
    tile_end = jnp.cumsum(tiles_per)
    row_start = (tile_end - tiles_per) * tm
    slot_start = jnp.cumsum(counts) - counts
    n_tiles = n_slots // tm + E
    P = n_tiles * tm
    n_active = tile_end[-1:].astype(jnp.int32)
    tile_ids = jnp.arange(n_tiles, dtype=jnp.int32)
    tile_expert = jnp.sum(tile_ids[:, None] >= tile_end[None, :], axis=1).astype(jnp.int32)
    tile_expert = jnp.minimum(tile_expert, tile_expert[jnp.maximum(n_active[0] - 1, 0)])
    row_e = jnp.repeat(tile_expert, tm)
    rank = jnp.arange(P, dtype=jnp.int32) - row_start[row_e]
    row_ok = jnp.logical_and(jnp.repeat(tile_ids, tm) < n_active[0], rank < counts[row_e])
    row_slot = order[jnp.clip(slot_start[row_e] + rank, 0, n_slots - 1)]
    src_token = jnp.where(row_ok, row_slot // TOP_K, 0)
    row_weight = jnp.where(row_ok, top_p.reshape(-1)[row_slot], 0.0)
    sorted_pos = jnp.argsort(order).astype(jnp.int32)
    dest = row_start[flat_e] + sorted_pos - slot_start[flat_e]
    hs = jnp.take(h, src_token, axis=0)
    y = _ffn(hs, w_gate.astype(BF16), w_up.astype(BF16), w_down.astype(BF16),
             tile_expert, n_active, row_weight.reshape(P, 1), tm)
    dest = dest.reshape(T, TOP_K)
    return jnp.take(y, dest[:, 0], axis=0), jnp.take(y, dest[:, 1], axis=0)


def kernel(x, attn_norm_g, w_in, na_q_g, na_k_g, na_rpb, mla_q_norm_g, mla_kv_norm_g, mla_w_uq, mla_w_ukv, mla_q_g, mla_k_g, hy_short_w, hy_short_b, hy_w1, hy_b1, hy_w2, hy_b2, hy_w3, hy_freq, hy_bias, group_norm_g, w_out, ffn_norm_g, dense_w_gate, dense_w_up, dense_w_down, router_w, moe_w_gate, moe_w_up, moe_w_down):
    B, L, D = x.shape
    T = B * L
    depth = attn_norm_g.shape[0]
    assert L % (NA_K_ROWS * GRID_W) == 0 and w_in.shape[2] == IN_WIDTH
    x2 = x.reshape(T, D).astype(F32)
    f1, g_dft = _dft_tables(L)
    deltas = []
    for l in range(depth):
        x2, h = _add_norm(x2, deltas, attn_norm_g[l].astype(F32))
        w = w_in[l]
        w_in_p = jnp.concatenate(
            [w[:, :SRC_CKV], w[:, SRC_HY:], w[:, SRC_CKV:SRC_HY],
             jnp.zeros((D, IN_WIDTH_PAD - IN_WIDTH), w.dtype)], axis=1).astype(BF16)
        proj = _matmul(h, w_in_p, BF16, 1024)
        o_na = _neighborhood_attention(proj.reshape(B, L, IN_WIDTH_PAD), na_rpb[l],
                                       na_q_g[l], na_k_g[l]).reshape(T, NA_WIDTH)
        qn, qr, kn, kr, v = _mla_prep(proj, L, mla_q_norm_g[l], mla_kv_norm_g[l], mla_w_uq[l],
                                      mla_w_ukv[l], mla_q_g[l], mla_k_g[l])
        o_mla = _flash_attention(qn, qr, kn, kr, v, B, L).reshape(T, MLA_HEADS * MLA_V_DIM)
        o_hy = _hyena(proj, B, L, hy_short_w[l], hy_short_b[l], hy_w1[l], hy_b1[l], hy_w2[l],
                      hy_b2[l], hy_w3[l], hy_freq[l], hy_bias[l], f1, g_dft).reshape(T, HY_CH)
        mix = _mix_norm(o_na, o_mla, o_hy, group_norm_g[l])
        d_mix = _matmul(mix, w_out[l].astype(BF16), BF16, 1024)
        i = l // 2
        if l % 2 == 0:
            x2, h = _add_norm(x2, [d_mix], ffn_norm_g[l].astype(F32))
            deltas = [_dense_ffn(h, dense_w_gate[i], dense_w_up[i], dense_w_down[i])]
        else:
            wr = jnp.pad(router_w[i].astype(F32), ((0, 0), (0, LANES - N_EXPERTS)))
            x2, h, logits = _add_norm(x2, [d_mix], ffn_norm_g[l].astype(F32), wr)
            deltas = list(_moe_ffn(h, logits, moe_w_gate[i], moe_w_up[i], moe_w_down[i]))
    (x2,) = _add_norm(x2, deltas)
    return x2.reshape(B, L, D).astype(x.dtype)
```

```python
import functools
import math

import jax
import jax.numpy as jnp
import numpy as np
from jax import lax
from jax.experimental import pallas as pl
from jax.experimental.pallas import tpu as pltpu

F32 = jnp.float32
BF16 = jnp.bfloat16

GRID_W = 64
NA_HEADS = 8
NA_HEAD_DIM = 64
NA_WIDTH = NA_HEADS * NA_HEAD_DIM
NA_WIN_ROWS = 8
NA_WIN_COLS = 16
MLA_HEADS = 8
MLA_NOPE_DIM = 128
MLA_ROPE_DIM = 64
MLA_V_DIM = 128
MLA_QK_DIM = MLA_NOPE_DIM + MLA_ROPE_DIM
MLA_Q_RANK = 512
MLA_KV_RANK = 256
ROPE_THETA = 10000.0
HY_CH = 512
HY_ORDER = 2
HY_POS_BANDS = 16
HY_DECAY_TARGET = 1e-2
HY_DECAY_FAST = 0.3
HY_DECAY_SLOW = 1.5
HY_MAX_DECAY = math.log(HY_DECAY_TARGET) / HY_DECAY_FAST
HY_MIN_DECAY = math.log(HY_DECAY_TARGET) / HY_DECAY_SLOW
N_EXPERTS = 8
TOP_K = 2
EPS = 1e-6
NEG_INF = -1e30

SRC_KROPE = 3 * NA_WIDTH + MLA_Q_RANK + MLA_KV_RANK
SRC_HY = SRC_KROPE + MLA_ROPE_DIM
IN_WIDTH = SRC_HY + (HY_ORDER + 1) * HY_CH
SRC_CKV = 3 * NA_WIDTH + MLA_Q_RANK
COL_NA_Q = 0
COL_NA_K = NA_WIDTH
COL_NA_V = 2 * NA_WIDTH
COL_CQ = 3 * NA_WIDTH
COL_HY = COL_CQ + MLA_Q_RANK
COL_CKV = COL_HY + (HY_ORDER + 1) * HY_CH
COL_KROPE = COL_CKV + MLA_KV_RANK
LANES = 128
IN_WIDTH_PAD = 4096

VMEM_LIMIT_BYTES = 56 * 1024 * 1024

NA_Q_ROWS = 8
NA_K_ROWS = 16


def _params(*sem):
    return pltpu.CompilerParams(dimension_semantics=sem, vmem_limit_bytes=VMEM_LIMIT_BYTES)


def _add_norm_kernel(*refs, n_delta, with_router, with_norm):
    x_ref = refs[0]
    d_refs = refs[1:1 + n_delta]
    pos = 1 + n_delta
    x = x_ref[...]
    for d in d_refs:
        x = x + d[...].astype(F32)
    if not with_norm:
        refs[pos][...] = x
        return
    g_ref = refs[pos]
    pos += 1
    if with_router:
        wr_ref = refs[pos]
        pos += 1
    xo_ref, h_ref = refs[pos], refs[pos + 1]
    xo_ref[...] = x
    h = x * lax.rsqrt(jnp.mean(x * x, axis=-1, keepdims=True) + EPS) * g_ref[...]
    h_ref[...] = h.astype(BF16)
    if with_router:
        refs[pos + 2][...] = jnp.dot(h, wr_ref[...], preferred_element_type=F32,
                                     precision=lax.Precision.HIGHEST)


def _add_norm(x, deltas, gain=None, router_w=None):
    T, D = x.shape
    tm = min(512, T)
    with_norm = gain is not None
    with_router = router_w is not None
    row = pl.BlockSpec((tm, D), lambda i: (i, 0))
    in_specs = [row] + [row] * len(deltas)
    args = [x] + list(deltas)
    out_shape = [jax.ShapeDtypeStruct((T, D), F32)]
    out_specs = [row]
    if with_norm:
        in_specs.append(pl.BlockSpec((1, D), lambda i: (0, 0)))
        args.append(gain.reshape(1, D))
        if with_router:
            in_specs.append(pl.BlockSpec((D, LANES), lambda i: (0, 0)))
            args.append(router_w)
        out_shape.append(jax.ShapeDtypeStruct((T, D), BF16))
        out_specs.append(row)
        if with_router:
            out_shape.append(jax.ShapeDtypeStruct((T, LANES), F32))
            out_specs.append(pl.BlockSpec((tm, LANES), lambda i: (i, 0)))
    out = pl.pallas_call(
        functools.partial(_add_norm_kernel, n_delta=len(deltas), with_router=with_router,
                          with_norm=with_norm),
        grid=(T // tm,), in_specs=in_specs, out_specs=out_specs, out_shape=out_shape,
        compiler_params=_params("parallel"), name="add_norm")(*args)
    return out


def _mm_kernel(a_ref, b_ref, o_ref):
    o_ref[...] = jnp.dot(a_ref[...], b_ref[...], preferred_element_type=F32).astype(o_ref.dtype)


def _matmul(a, b, out_dtype, tn):
    M, K = a.shape
    N = b.shape[1]
    tm = min(1024, M)
    return pl.pallas_call(
        _mm_kernel, grid=(M // tm, N // tn),
        in_specs=[pl.BlockSpec((tm, K), lambda i, j: (i, 0)),
                  pl.BlockSpec((K, tn), lambda i, j: (0, j))],
        out_specs=pl.BlockSpec((tm, tn), lambda i, j: (i, j)),
        out_shape=jax.ShapeDtypeStruct((M, N), out_dtype),
        compiler_params=_params("parallel", "parallel"), name="matmul")(a, b)


def _na_bias_table(rpb, rows):
    n_groups = rows // NA_Q_ROWS
    reps = (0, min(1, n_groups - 1), n_groups - 1)
    row_off = np.zeros((3, NA_Q_ROWS, NA_K_ROWS), np.int32)
    row_ok = np.zeros((3, NA_Q_ROWS, NA_K_ROWS), bool)
    for p, g in enumerate(reps):
        start = int(np.clip(g * NA_Q_ROWS - NA_WIN_ROWS // 2, 0, rows - NA_K_ROWS))
        for i in range(NA_Q_ROWS):
            r = g * NA_Q_ROWS + i
            rs = int(np.clip(r - NA_WIN_ROWS // 2, 0, rows - NA_WIN_ROWS))
            for j in range(NA_K_ROWS):
                kr = start + j
                ok = rs <= kr < rs + NA_WIN_ROWS
                row_ok[p, i, j] = ok
                row_off[p, i, j] = np.clip(kr - r + NA_WIN_ROWS - 1, 0, 2 * NA_WIN_ROWS - 2)
    cols = np.arange(GRID_W)
    col_start = np.clip(cols - NA_WIN_COLS // 2, 0, GRID_W - NA_WIN_COLS)
    col_ok = (cols[None, :] >= col_start[:, None]) & (cols[None, :] < col_start[:, None] + NA_WIN_COLS)
    col_off = np.clip(cols[None, :] - cols[:, None] + NA_WIN_COLS - 1, 0, 2 * NA_WIN_COLS - 2)
    row_sel = np.eye(2 * NA_WIN_ROWS - 1, dtype=np.float32)[row_off]
    col_sel = np.eye(2 * NA_WIN_COLS - 1, dtype=np.float32)[col_off]
    tab = jnp.einsum('hrc,pijr,abc->hpiajb', rpb.astype(F32), row_sel, col_sel,
                     precision=lax.Precision.HIGHEST)
    ok = row_ok[:, :, None, :, None] & col_ok[None, None, :, None, :]
    tab = jnp.where(ok[None], tab, NEG_INF)
    return tab.reshape(NA_HEADS, 3, NA_Q_ROWS * GRID_W, NA_K_ROWS * GRID_W)


def _na_kernel(q_ref, k_ref, v_ref, bias_ref, gq_ref, gk_ref, ones_ref, o_ref, *, rows):
    g = pl.program_id(2)
    tq = NA_Q_ROWS * GRID_W
    tk = NA_K_ROWS * GRID_W
    start = jnp.clip(g * NA_Q_ROWS - NA_WIN_ROWS // 2, 0, rows - NA_K_ROWS) * GRID_W
    start = pl.multiple_of(start, GRID_W)
    ones_bd = ones_ref[...]

    def head_norm(x, gain):
        x2 = x * x
        hi = x2.astype(BF16)
        lo = (x2 - hi.astype(F32)).astype(BF16)
        ssq = (jnp.dot(hi, ones_bd, preferred_element_type=F32)
               + jnp.dot(lo, ones_bd, preferred_element_type=F32))
        return x * lax.rsqrt(ssq * (1.0 / NA_HEAD_DIM) + EPS) * gain

    q = head_norm(q_ref[0].astype(F32), gq_ref[...]) * (NA_HEAD_DIM ** -0.5)
    kw = head_norm(k_ref[0, pl.ds(start, tk), :].astype(F32), gk_ref[...]).astype(BF16)
    vw = v_ref[0, pl.ds(start, tk), :]
    lane = lax.broadcasted_iota(jnp.int32, (tq, LANES), 1)
    outs = []
    for hh in range(2):
        sel = (lane < NA_HEAD_DIM) if hh == 0 else (lane >= NA_HEAD_DIM)
        qm = jnp.where(sel, q, 0.0).astype(BF16)
        s = lax.dot_general(qm, kw, (((1,), (1,)), ((), ())), preferred_element_type=F32)
        s = s + bias_ref[hh, 0]
        m = jnp.max(s, axis=-1, keepdims=True)
        p = jnp.exp(s - m)
        l = jnp.sum(p, axis=-1, keepdims=True)
        o = jnp.dot(p.astype(BF16), vw, preferred_element_type=F32)
        outs.append(o / l)
    o_ref[0] = jnp.where(lane < NA_HEAD_DIM, outs[0], outs[1]).astype(o_ref.dtype)


def _neighborhood_attention(proj3, rpb, q_g, k_g):
    B, L, _ = proj3.shape
    rows = L // GRID_W
    n_groups = rows // NA_Q_ROWS
    tq = NA_Q_ROWS * GRID_W
    tk = NA_K_ROWS * GRID_W
    bias = _na_bias_table(rpb, rows)
    gq = jnp.tile(q_g.astype(F32), 2).reshape(1, LANES)
    gk = jnp.tile(k_g.astype(F32), 2).reshape(1, LANES)
    ones_bd = jnp.asarray(np.kron(np.eye(2), np.ones((NA_HEAD_DIM, NA_HEAD_DIM))), BF16)
    qb, kb, vb = COL_NA_Q // LANES, COL_NA_K // LANES, COL_NA_V // LANES

    def pattern(g):
        return (g > 0).astype(jnp.int32) + (g == n_groups - 1).astype(jnp.int32)

    return pl.pallas_call(
        functools.partial(_na_kernel, rows=rows),
        grid=(NA_HEADS // 2, B, n_groups),
        in_specs=[
            pl.BlockSpec((1, tq, LANES), lambda hp, b, g: (b, g, qb + hp)),
            pl.BlockSpec((1, L, LANES), lambda hp, b, g: (b, 0, kb + hp)),
            pl.BlockSpec((1, L, LANES), lambda hp, b, g: (b, 0, vb + hp)),
            pl.BlockSpec((2, 1, tq, tk), lambda hp, b, g: (hp, pattern(g), 0, 0)),
            pl.BlockSpec((1, LANES), lambda hp, b, g: (0, 0)),
            pl.BlockSpec((1, LANES), lambda hp, b, g: (0, 0)),
            pl.BlockSpec((LANES, LANES), lambda hp, b, g: (0, 0)),
        ],
        out_specs=pl.BlockSpec((1, tq, LANES), lambda hp, b, g: (b, g, hp)),
        out_shape=jax.ShapeDtypeStruct((B, L, NA_WIDTH), BF16),
        compiler_params=_params("parallel", "parallel", "arbitrary"),
        name="neighborhood_attention")(proj3, proj3, proj3, bias, gq, gk, ones_bd)


def _mla_prep_kernel(cq_ref, ckv_ref, krope_ref, gqn_ref, gkvn_ref, wuq_ref, wukv_ref,
                     qgn_ref, qgr_ref, kgn_ref, kgr_ref, cos_ref, sin_ref,
                     qn_ref, qr_ref, kn_ref, kr_ref, v_ref):
    def rms(x, g):
        return x * lax.rsqrt(jnp.mean(x * x, axis=-1, keepdims=True) + EPS) * g

    tm = cq_ref.shape[0]
    nope_w = MLA_HEADS * MLA_NOPE_DIM
    cq = rms(cq_ref[...].astype(F32), gqn_ref[...]).astype(BF16)
    q = jnp.dot(cq, wuq_ref[...], preferred_element_type=F32)
    ckv = rms(ckv_ref[...].astype(F32), gkvn_ref[...]).astype(BF16)
    kv = jnp.dot(ckv, wukv_ref[...], preferred_element_type=F32)
    v_ref[...] = kv[:, nope_w:].astype(BF16)

    lane = lax.broadcasted_iota(jnp.int32, (tm, LANES), 1)
    first = lane < MLA_ROPE_DIM
    low_half = (lane % MLA_ROPE_DIM) < (MLA_ROPE_DIM // 2)
    cosv = cos_ref[...]
    sinv = sin_ref[...]
    scale = (MLA_QK_DIM ** -0.5) * math.log2(math.e)
    inv_dim = 1.0 / MLA_QK_DIM

    def rope(x):
        swapped = jnp.where(low_half, pltpu.roll(x, LANES - MLA_ROPE_DIM // 2, 1),
                            pltpu.roll(x, MLA_ROPE_DIM // 2, 1))
        return x * cosv + swapped * sinv

    qgn, qgr = qgn_ref[...], qgr_ref[...]
    kgn, kgr = kgn_ref[...], kgr_ref[...]
    kraw = krope_ref[...].astype(F32)
    kdup = jnp.where(first, kraw, pltpu.roll(kraw, MLA_ROPE_DIM, 1))
    k_rope_sq = jnp.where(first, kraw * kraw, 0.0)
    k_roped = rope(kdup * kgr)

    for j in range(MLA_HEADS // 2):
        qr_j = q[:, nope_w + LANES * j:nope_w + LANES * (j + 1)]
        qr_sq = qr_j * qr_j
        q_rs, k_rs = [], []
        for hh in range(2):
            h = 2 * j + hh
            cols = slice(LANES * h, LANES * (h + 1))
            qn_h = q[:, cols]
            t = qn_h * qn_h + jnp.where(first if hh == 0 else jnp.logical_not(first), qr_sq, 0.0)
            rs = lax.rsqrt(jnp.sum(t, axis=-1, keepdims=True) * inv_dim + EPS)
            qn_ref[:, cols] = (qn_h * rs * qgn * scale).astype(BF16)
            q_rs.append(rs)
            kn_h = kv[:, cols]
            t = kn_h * kn_h + k_rope_sq
            rs = lax.rsqrt(jnp.sum(t, axis=-1, keepdims=True) * inv_dim + EPS)
            kn_ref[:, cols] = (kn_h * rs * kgn).astype(BF16)
            k_rs.append(rs)
        pair = slice(LANES * j, LANES * (j + 1))
        qr_ref[:, pair] = (rope(qr_j * jnp.where(first, q_rs[0], q_rs[1]) * qgr) * scale).astype(BF16)
        kr_ref[:, pair] = (k_roped * jnp.where(first, k_rs[0], k_rs[1])).astype(BF16)


def _mla_prep(proj, L, q_norm_g, kv_norm_g, w_uq, w_ukv, q_head_g, k_head_g):
    T = proj.shape[0]
    tm = min(512, L)
    H = MLA_HEADS
    wq = w_uq.reshape(MLA_Q_RANK, H, MLA_QK_DIM)
    wq = jnp.concatenate([wq[:, :, :MLA_NOPE_DIM].reshape(MLA_Q_RANK, -1),
                          wq[:, :, MLA_NOPE_DIM:].reshape(MLA_Q_RANK, -1)], axis=1).astype(BF16)
    wkv = w_ukv.reshape(MLA_KV_RANK, H, MLA_NOPE_DIM + MLA_V_DIM)
    wkv = jnp.concatenate([wkv[:, :, :MLA_NOPE_DIM].reshape(MLA_KV_RANK, -1),
                           wkv[:, :, MLA_NOPE_DIM:].reshape(MLA_KV_RANK, -1)], axis=1).astype(BF16)
    half = MLA_ROPE_DIM // 2
    freqs = ROPE_THETA ** (-jnp.arange(half, dtype=F32) / half)
    ang = jnp.arange(L).astype(F32)[:, None] * freqs[None, :]
    cos, sin = jnp.cos(ang), jnp.sin(ang)
    cos2 = jnp.tile(jnp.concatenate([cos, cos], axis=1), (1, 2))
    sin2 = jnp.tile(jnp.concatenate([-sin, sin], axis=1), (1, 2))

    def vec(g, reps=1):
        return jnp.tile(g.astype(F32), reps).reshape(1, -1)

    const = lambda shape: pl.BlockSpec(shape, lambda i: (0, 0))
    nblk = L // tm
    outs = pl.pallas_call(
        _mla_prep_kernel, grid=(T // tm,),
        in_specs=[
            pl.BlockSpec((tm, MLA_Q_RANK), lambda i: (i, COL_CQ // MLA_Q_RANK)),
            pl.BlockSpec((tm, MLA_KV_RANK), lambda i: (i, COL_CKV // MLA_KV_RANK)),
            pl.BlockSpec((tm, LANES), lambda i: (i, COL_KROPE // LANES)),
            const((1, MLA_Q_RANK)), const((1, MLA_KV_RANK)),
            const(wq.shape), const(wkv.shape),
            const((1, LANES)), const((1, LANES)), const((1, LANES)), const((1, LANES)),
            pl.BlockSpec((tm, LANES), lambda i: (i % nblk, 0)),
            pl.BlockSpec((tm, LANES), lambda i: (i % nblk, 0)),
        ],
        out_specs=[
            pl.BlockSpec((tm, H * MLA_NOPE_DIM), lambda i: (i, 0)),
            pl.BlockSpec((tm, H * MLA_ROPE_DIM), lambda i: (i, 0)),
            pl.BlockSpec((tm, H * MLA_NOPE_DIM), lambda i: (i, 0)),
            pl.BlockSpec((tm, H * MLA_ROPE_DIM), lambda i: (i, 0)),
            pl.BlockSpec((tm, H * MLA_V_DIM), lambda i: (i, 0)),
        ],
        out_shape=[
            jax.ShapeDtypeStruct((T, H * MLA_NOPE_DIM), BF16),
            jax.ShapeDtypeStruct((T, H * MLA_ROPE_DIM), BF16),
            jax.ShapeDtypeStruct((T, H * MLA_NOPE_DIM), BF16),
            jax.ShapeDtypeStruct((T, H * MLA_ROPE_DIM), BF16),
            jax.ShapeDtypeStruct((T, H * MLA_V_DIM), BF16),
        ],
        compiler_params=_params("parallel"), name="mla_prep")(
            proj, proj, proj, vec(q_norm_g), vec(kv_norm_g), wq, wkv,
            vec(q_head_g[:MLA_NOPE_DIM]), vec(q_head_g[MLA_NOPE_DIM:], 2),
            vec(k_head_g[:MLA_NOPE_DIM]), vec(k_head_g[MLA_NOPE_DIM:], 2), cos2, sin2)
    return outs


FLASH_STREAM_ROWS = 256


def _flash_kernel(qn_ref, qr_ref, kn0_ref, kr0_ref, kn1_ref, kr1_ref, v_ref, o_ref,
                  q_sc, s_even, s_odd, m_sc, acc_sc):
    h = pl.program_id(1)
    ki = pl.program_id(3)
    tq = qn_ref.shape[1]
    tk = kn0_ref.shape[1]
    rows = min(FLASH_STREAM_ROWS, tq)
    n_streams = tq // rows
    nt = (((1,), (1,)), ((), ()))

    @pl.when(ki == 0)
    def _():
        lane = lax.broadcasted_iota(jnp.int32, (tq, LANES), 1)
        own = (lane // MLA_ROPE_DIM) == (h % 2)
        q_sc[:, :LANES] = qn_ref[0]
        q_sc[:, LANES:] = jnp.where(own, qr_ref[0], jnp.zeros_like(qr_ref[0]))
        m_sc[...] = jnp.full_like(m_sc, NEG_INF)
        acc_sc[...] = jnp.zeros_like(acc_sc)
        k0 = jnp.concatenate([kn0_ref[0], kr0_ref[0]], axis=-1)
        for r in range(n_streams):
            sl = slice(r * rows, (r + 1) * rows)
            s_even[sl, :] = lax.dot_general(q_sc[sl, :], k0, nt, preferred_element_type=F32)

    def step(s_cur, s_next):
        k1 = jnp.concatenate([kn1_ref[0], kr1_ref[0]], axis=-1)
        v_ones = jnp.concatenate([v_ref[0], jnp.ones((tk, LANES), BF16)], axis=-1)
        for r in range(n_streams):
            sl = slice(r * rows, (r + 1) * rows)
            s_next[sl, :] = lax.dot_general(q_sc[sl, :], k1, nt, preferred_element_type=F32)
            s = s_cur[sl, :]
            m_prev = m_sc[sl, :]
            m_new = jnp.maximum(m_prev, jnp.max(s, axis=-1, keepdims=True))
            p = jnp.exp2(s - m_new).astype(BF16)
            acc_sc[sl, :] = (jnp.exp2(m_prev - m_new) * acc_sc[sl, :]
                             + jnp.dot(p, v_ones, preferred_element_type=F32))
            m_sc[sl, :] = m_new

    @pl.when(ki % 2 == 0)
    def _():
        step(s_even, s_odd)

    @pl.when(ki % 2 == 1)
    def _():
        step(s_odd, s_even)

    @pl.when(ki == pl.num_programs(3) - 1)
    def _():
        o_ref[0] = (acc_sc[:, :LANES] / acc_sc[:, LANES:]).astype(o_ref.dtype)


def _flash_attention(qn, qr, kn, kr, v, B, L):
    H = MLA_HEADS
    tq = min(2048, L)
    tk = min(1024, L)
    r3 = lambda a: a.reshape(B, L, a.shape[-1])
    nk = L // tk
    ahead = lambda ki: jnp.minimum(ki + 1, nk - 1)
    return pl.pallas_call(
        _flash_kernel, grid=(B, H, L // tq, nk),
        in_specs=[
            pl.BlockSpec((1, tq, LANES), lambda b, h, qi, ki: (b, qi, h)),
            pl.BlockSpec((1, tq, LANES), lambda b, h, qi, ki: (b, qi, h // 2)),
            pl.BlockSpec((1, tk, LANES), lambda b, h, qi, ki: (b, 0, h)),
            pl.BlockSpec((1, tk, LANES), lambda b, h, qi, ki: (b, 0, h // 2)),
            pl.BlockSpec((1, tk, LANES), lambda b, h, qi, ki: (b, ahead(ki), h)),
            pl.BlockSpec((1, tk, LANES), lambda b, h, qi, ki: (b, ahead(ki), h // 2)),
            pl.BlockSpec((1, tk, LANES), lambda b, h, qi, ki: (b, ki, h)),
        ],
        out_specs=pl.BlockSpec((1, tq, LANES), lambda b, h, qi, ki: (b, qi, h)),
        out_shape=jax.ShapeDtypeStruct((B, L, H * MLA_V_DIM), BF16),
        scratch_shapes=[pltpu.VMEM((tq, 2 * LANES), BF16), pltpu.VMEM((tq, tk), F32),
                        pltpu.VMEM((tq, tk), F32), pltpu.VMEM((tq, 1), F32),
                        pltpu.VMEM((tq, 2 * LANES), F32)],
        compiler_params=_params("parallel", "parallel", "parallel", "arbitrary"),
        name="mla_flash")(r3(qn), r3(qr), r3(kn), r3(kr), r3(kn), r3(kr), r3(v))


def _hyena_filters(L, w1, b1, w2, b2, w3, freq):
    t = jnp.linspace(0.0, 1.0, L, dtype=F32)[:, None]
    bands = jnp.linspace(1e-4, HY_POS_BANDS - 1, HY_POS_BANDS, dtype=F32)[None, :]
    w = 2.0 * math.pi * jnp.arange(L, dtype=F32)[:, None] / L
    z = jnp.concatenate([t, jnp.cos(bands * w), -jnp.sin(bands * w)], axis=-1)
    hp = lax.Precision.HIGHEST
    deltas = jnp.abs(jnp.linspace(HY_MIN_DECAY, HY_MAX_DECAY, HY_CH, dtype=F32))
    w3d = w3.astype(F32).reshape(w3.shape[0], 2, HY_ORDER * HY_CH)

    def direction(z, t, d):
        a = jnp.sin(freq[0] * (jnp.matmul(z, w1, precision=hp) + b1))
        a = jnp.sin(freq[1] * (jnp.matmul(a, w2, precision=hp) + b2))
        f = jnp.matmul(a, w3d[:, d], precision=hp).reshape(L, HY_ORDER, HY_CH)
        return f * jnp.exp(-t * deltas[None, :])[:, None, :]

    fwd = direction(z, t, 0)
    bwd_rev = direction(z[::-1], t[::-1], 1)[:L - 1]
    k = jnp.concatenate([fwd, jnp.zeros((1, HY_ORDER, HY_CH), F32), bwd_rev], axis=0)
    return k * lax.rsqrt(jnp.sum(k * k, axis=0, keepdims=True) + EPS)


HALO_ROWS = 16


def _short_conv_kernel(x_ref, xp_ref, xn_ref, w_ref, b_ref, o_ref, *, tiles_per_seq):
    i = pl.program_id(0)
    x = x_ref[...].astype(F32)
    tm = x.shape[0]
    row = lax.broadcasted_iota(jnp.int32, x.shape, 0)
    t = i % tiles_per_seq
    halo_prev = jnp.where(t == 0, 0.0, xp_ref[HALO_ROWS - 1:HALO_ROWS, :].astype(F32))
    halo_next = jnp.where(t == tiles_per_seq - 1, 0.0, xn_ref[0:1, :].astype(F32))
    prev = jnp.where(row == 0, halo_prev, pltpu.roll(x, 1, 0))
    nxt = jnp.where(row == tm - 1, halo_next, pltpu.roll(x, tm - 1, 0))
    o_ref[...] = prev * w_ref[0:1, :] + x * w_ref[1:2, :] + nxt * w_ref[2:3, :] + b_ref[...]


def _short_conv(proj, L, w, b):
    T = proj.shape[0]
    W = (HY_ORDER + 1) * HY_CH
    tm = min(1024, L)
    tc = 512
    c0 = COL_HY // tc
    hb = tm // HALO_ROWS
    n_halo = T // HALO_ROWS
    return pl.pallas_call(
        functools.partial(_short_conv_kernel, tiles_per_seq=L // tm),
        grid=(T // tm, W // tc),
        in_specs=[
            pl.BlockSpec((tm, tc), lambda i, c: (i, c0 + c)),
            pl.BlockSpec((HALO_ROWS, tc), lambda i, c: (jnp.maximum(i * hb - 1, 0), c0 + c)),
            pl.BlockSpec((HALO_ROWS, tc), lambda i, c: (jnp.minimum((i + 1) * hb, n_halo - 1), c0 + c)),
            pl.BlockSpec((3, tc), lambda i, c: (0, c)),
            pl.BlockSpec((1, tc), lambda i, c: (0, c)),
        ],
        out_specs=pl.BlockSpec((tm, tc), lambda i, c: (i, c)),
        out_shape=jax.ShapeDtypeStruct((T, W), F32),
        compiler_params=_params("parallel", "parallel"), name="hyena_short_conv")(
            proj, proj, proj, w.astype(F32), b.astype(F32).reshape(1, W))


HY_CB = LANES
HY_NLO = 128
HY_UNROLL = 4
SUBLANES = 8


def _dft_geometry(L):
    nhi = 2 * L // HY_NLO
    n_kb = nhi // 2 + 1
    kb_pad = -(-n_kb // SUBLANES) * SUBLANES
    chunk = max(d for d in range(1, 17) if n_kb % d == 0)
    return nhi, n_kb, kb_pad, chunk


def _dft_tables(L):
    N = 2 * L
    nhi, n_kb, kb_pad, _ = _dft_geometry(L)
    kb = jnp.arange(kb_pad, dtype=jnp.int32)
    nh = jnp.arange(nhi, dtype=jnp.int32)
    ang = (2.0 * math.pi / nhi) * ((kb[:, None] * nh[None, :]) % nhi).astype(F32)
    cos, sin = jnp.cos(ang), jnp.sin(ang)
    f1 = jnp.concatenate([cos, -sin], axis=0)
    wgt = jnp.where((kb == 0) | (kb == nhi // 2), 1.0, jnp.where(kb < n_kb, 2.0, 0.0))[:, None]
    f1_inv = jnp.concatenate([cos * wgt, -sin * wgt], axis=0)[:, :nhi // 2].T
    ka = jnp.arange(HY_NLO, dtype=jnp.int32)
    idx = (ka[None, None, :] * (ka[None, :, None] * nhi + kb[:n_kb, None, None])) % N
    ang = (2.0 * math.pi / N) * idx.astype(F32)
    gr, gi = jnp.cos(ang), -jnp.sin(ang)
    g = jnp.concatenate([jnp.concatenate([gr, -gi], axis=2),
                         jnp.concatenate([gi, gr], axis=2)], axis=1)
    return f1.astype(BF16), f1_inv.astype(BF16), g.astype(BF16)


def _dft_major_stage(src_ref, f1, a_ref, n_rows):
    two_nhi = f1.shape[0]

    def body(n_lo, carry):
        xs = src_ref[pl.ds(n_lo, n_rows, stride=HY_NLO), :]
        dst = pl.multiple_of(n_lo * two_nhi, two_nhi)
        a_ref[pl.ds(dst, two_nhi), :] = jnp.dot(f1, xs.astype(BF16), preferred_element_type=F32)
        return carry

    lax.fori_loop(0, HY_NLO, body, 0, unroll=HY_UNROLL)


def _load_slab(a_ref, kb, kb_pad):
    re = a_ref[pl.ds(kb, HY_NLO, stride=2 * kb_pad), :]
    im = a_ref[pl.ds(kb_pad + kb, HY_NLO, stride=2 * kb_pad), :]
    return jnp.concatenate([re, im], axis=0).astype(BF16)


def _hyena_spectrum_kernel(k_ref, f1_ref, g_ref, o_ref, a_ref, *, nhi, kb_pad, chunk):
    j = pl.program_id(1)

    @pl.when(j == 0)
    def _():
        _dft_major_stage(k_ref, f1_ref[...], a_ref, nhi)

    def slab(s, carry):
        kb = j * chunk + s
        o_ref[0, s] = jnp.dot(g_ref[s], _load_slab(a_ref, kb, kb_pad), preferred_element_type=F32)
        return carry

    lax.fori_loop(0, chunk, slab, 0, unroll=min(HY_UNROLL, chunk))


def _hyena_spectrum(k2, f1, g):
    N, n_ch = k2.shape
    nhi, n_kb, kb_pad, chunk = _dft_geometry(N // 2)
    return pl.pallas_call(
        functools.partial(_hyena_spectrum_kernel, nhi=nhi, kb_pad=kb_pad, chunk=chunk),
        grid=(n_ch // HY_CB, n_kb // chunk),
        in_specs=[
            pl.BlockSpec((N, HY_CB), lambda c, j: (0, c)),
            pl.BlockSpec(f1.shape, lambda c, j: (0, 0)),
            pl.BlockSpec((chunk, 2 * HY_NLO, 2 * HY_NLO), lambda c, j: (j, 0, 0)),
        ],
        out_specs=pl.BlockSpec((1, chunk, 2 * HY_NLO, HY_CB), lambda c, j: (c, j, 0, 0)),
        out_shape=jax.ShapeDtypeStruct((n_ch // HY_CB, n_kb, 2 * HY_NLO, HY_CB), F32),
        scratch_shapes=[pltpu.VMEM((HY_NLO * 2 * kb_pad, HY_CB), F32)],
        compiler_params=_params("parallel", "arbitrary"), name="hyena_spectrum")(k2, f1, g)


def _hyena_conv_kernel(u_ref, gate_ref, bias_ref, f1_ref, f1t_ref, g_ref, kf_ref, o_ref,
                       a_ref, y_ref, *, n_in, kb_pad, chunk):
    j = pl.program_id(2)
    L = n_in * HY_NLO

    @pl.when(j == 0)
    def _():
        _dft_major_stage(u_ref.at[0], f1_ref[...], a_ref, n_in)

    def forward(s, carry):
        kb = j * chunk + s
        t = jnp.dot(g_ref[s], _load_slab(a_ref, kb, kb_pad), preferred_element_type=F32)
        kf = kf_ref[0, s]
        tr, ti = t[:HY_NLO], t[HY_NLO:]
        kr, ki = kf[:HY_NLO], kf[HY_NLO:]
        y_ref[s] = jnp.concatenate([tr * kr - ti * ki, tr * ki + ti * kr], axis=0).astype(BF16)
        return carry

    lax.fori_loop(0, chunk, forward, 0, unroll=min(HY_UNROLL, chunk))

    def inverse(s, carry):
        kb = j * chunk + s
        r = lax.dot_general(g_ref[s], y_ref[s], (((0,), (0,)), ((), ())),
                            preferred_element_type=F32)
        a_ref[pl.ds(kb, HY_NLO, stride=2 * kb_pad), :] = r[:HY_NLO]
        a_ref[pl.ds(kb_pad + kb, HY_NLO, stride=2 * kb_pad), :] = r[HY_NLO:]
        return carry

    lax.fori_loop(0, chunk, inverse, 0, unroll=min(HY_UNROLL, chunk))

    @pl.when(j == pl.num_programs(2) - 1)
    def _():
        f1t = f1t_ref[...]

        def body(n_lo, carry):
            src = pl.multiple_of(n_lo * 2 * kb_pad, 2 * kb_pad)
            blk = a_ref[pl.ds(src, 2 * kb_pad), :].astype(BF16)
            o_ref[0, pl.ds(n_lo, n_in, stride=HY_NLO), :] = jnp.dot(
                f1t, blk, preferred_element_type=F32)
            return carry

        lax.fori_loop(0, HY_NLO, body, 0, unroll=HY_UNROLL)
        inv_n = 1.0 / (2 * L)
        rows = 512

        def gate_rows(c, carry):
            r0 = pl.multiple_of(c * rows, rows)
            sl = pl.ds(r0, rows)
            o_ref[0, sl, :] = gate_ref[0, sl, :] * (o_ref[0, sl, :] * inv_n
                                                    + u_ref[0, sl, :] * bias_ref[...])
            return carry

        lax.fori_loop(0, L // rows, gate_rows, 0)


def _hyena_conv(u3, u_col, gate3, gate_col, bias, kf, kf_row, f1, f1_inv, g):
    B, L, _ = u3.shape
    nhi, n_kb, kb_pad, chunk = _dft_geometry(L)
    n_in = nhi // 2
    f1_in = f1[:, :n_in]
    n_cb = HY_CH // HY_CB
    return pl.pallas_call(
        functools.partial(_hyena_conv_kernel, n_in=n_in, kb_pad=kb_pad, chunk=chunk),
        grid=(B, n_cb, n_kb // chunk),
        in_specs=[
            pl.BlockSpec((1, L, HY_CB), lambda b, c, j: (b, 0, u_col + c)),
            pl.BlockSpec((1, L, HY_CB), lambda b, c, j: (b, 0, gate_col + c)),
            pl.BlockSpec((1, HY_CB), lambda b, c, j: (0, c)),
            pl.BlockSpec(f1_in.shape, lambda b, c, j: (0, 0)),
            pl.BlockSpec(f1_inv.shape, lambda b, c, j: (0, 0)),
            pl.BlockSpec((chunk, 2 * HY_NLO, 2 * HY_NLO), lambda b, c, j: (j, 0, 0)),
            pl.BlockSpec((1, chunk, 2 * HY_NLO, HY_CB), lambda b, c, j: (kf_row + c, j, 0, 0)),
        ],
        out_specs=pl.BlockSpec((1, L, HY_CB), lambda b, c, j: (b, 0, c)),
        out_shape=jax.ShapeDtypeStruct((B, L, HY_CH), F32),
        scratch_shapes=[pltpu.VMEM((HY_NLO * 2 * kb_pad, HY_CB), F32),
                        pltpu.VMEM((chunk, 2 * HY_NLO, HY_CB), BF16)],
        compiler_params=_params("parallel", "parallel", "arbitrary"), name="hyena_long_conv")(
            u3, gate3, bias.astype(F32).reshape(1, HY_CH), f1_in, f1_inv, g, kf)


def _hyena(proj, B, L, short_w, short_b, w1, b1, w2, b2, w3, freq, bias, dft):
    f1, f1_inv, g = dft
    u = _short_conv(proj, L, short_w, short_b).reshape(B, L, (HY_ORDER + 1) * HY_CH)
    k = _hyena_filters(L, w1, b1, w2, b2, w3, freq).reshape(2 * L, HY_ORDER * HY_CH)
    kf = _hyena_spectrum(k, f1, g)
    n_cb = HY_CH // HY_CB
    z = _hyena_conv(u, 0, u, n_cb, bias[0], kf, 0, f1, f1_inv, g)
    return _hyena_conv(z, 0, u, 2 * n_cb, bias[1], kf, n_cb, f1, f1_inv, g)


def _mix_norm_kernel(na_ref, mla_ref, hy_ref, g_ref, o_ref):
    def rms(x, g):
        return (x * lax.rsqrt(jnp.mean(x * x, axis=-1, keepdims=True) + EPS) * g).astype(BF16)

    a, b = NA_WIDTH, NA_WIDTH + MLA_HEADS * MLA_V_DIM
    o_ref[:, :a] = rms(na_ref[...].astype(F32), g_ref[:, :a])
    o_ref[:, a:b] = rms(mla_ref[...].astype(F32), g_ref[:, a:b])
    o_ref[:, b:] = rms(hy_ref[...].astype(F32), g_ref[:, b:])


def _mix_norm(o_na, o_mla, o_hy, gain):
    T = o_na.shape[0]
    W = o_na.shape[1] + o_mla.shape[1] + o_hy.shape[1]
    tm = min(1024, T)
    row = lambda a: pl.BlockSpec((tm, a.shape[1]), lambda i: (i, 0))
    return pl.pallas_call(
        _mix_norm_kernel, grid=(T // tm,),
        in_specs=[row(o_na), row(o_mla), row(o_hy), pl.BlockSpec((1, W), lambda i: (0, 0))],
        out_specs=pl.BlockSpec((tm, W), lambda i: (i, 0)),
        out_shape=jax.ShapeDtypeStruct((T, W), BF16),
        compiler_params=_params("parallel"), name="mix_norm")(
            o_na, o_mla, o_hy, gain.astype(F32).reshape(1, W))


def _ffn_kernel(te_ref, na_ref, h_ref, wg_ref, wu_ref, wd_ref, rw_ref, o_ref, acc_ref):
    i = pl.program_id(0)
    j = pl.program_id(1)
    last = pl.num_programs(1) - 1
    active = i < na_ref[0]

    @pl.when(jnp.logical_and(active, j == 0))
    def _():
        acc_ref[...] = jnp.zeros_like(acc_ref)

    @pl.when(active)
    def _():
        h = h_ref[...]
        g = jnp.dot(h, wg_ref[0], preferred_element_type=F32)
        u = jnp.dot(h, wu_ref[0], preferred_element_type=F32)
        a = (g * (1.0 / (1.0 + jnp.exp(-g))) * u).astype(BF16)
        acc_ref[...] += jnp.dot(a, wd_ref[0], preferred_element_type=F32)

    @pl.when(jnp.logical_and(active, j == last))
    def _():
        o_ref[...] = (acc_ref[...] * rw_ref[...]).astype(o_ref.dtype)

    @pl.when(jnp.logical_and(jnp.logical_not(active), j == last))
    def _():
        o_ref[...] = jnp.zeros_like(o_ref)


def _ffn(h, w_gate, w_up, w_down, tile_expert, n_active, row_weight, tm):
    P, D = h.shape
    F = w_gate.shape[2]
    tf = 512
    nf = F // tf

    def fidx(i, j, na):
        return jnp.where(i < na[0], j, nf - 1)

    grid_spec = pltpu.PrefetchScalarGridSpec(
        num_scalar_prefetch=2, grid=(P // tm, nf),
        in_specs=[
            pl.BlockSpec((tm, D), lambda i, j, te, na: (i, 0)),
            pl.BlockSpec((1, D, tf), lambda i, j, te, na: (te[i], 0, fidx(i, j, na))),
            pl.BlockSpec((1, D, tf), lambda i, j, te, na: (te[i], 0, fidx(i, j, na))),
            pl.BlockSpec((1, tf, D), lambda i, j, te, na: (te[i], fidx(i, j, na), 0)),
            pl.BlockSpec((tm, 1), lambda i, j, te, na: (i, 0)),
        ],
        out_specs=pl.BlockSpec((tm, D), lambda i, j, te, na: (i, 0)),
        scratch_shapes=[pltpu.VMEM((tm, D), F32)])
    return pl.pallas_call(
        _ffn_kernel, grid_spec=grid_spec,
        out_shape=jax.ShapeDtypeStruct((P, D), BF16),
        compiler_params=_params("parallel", "arbitrary"), name="swiglu_ffn")(
            tile_expert, n_active, h, w_gate, w_up, w_down, row_weight)


def _dense_ffn(h, w_gate, w_up, w_down):
    T = h.shape[0]
    tm = min(1024, T)
    n = T // tm
    return _ffn(h, w_gate[None].astype(BF16), w_up[None].astype(BF16), w_down[None].astype(BF16),
                jnp.zeros((n,), jnp.int32), jnp.full((1,), n, jnp.int32),
                jnp.ones((T, 1), F32), tm)


def _moe_ffn(h, logits, w_gate, w_up, w_down):
    T, D = h.shape
    E = N_EXPERTS
    tm = min(1024, T)
    probs = jax.nn.softmax(logits[:, :E], axis=-1)
    top_p, top_i = lax.top_k(probs, TOP_K)
    top_p = top_p / jnp.sum(top_p, axis=-1, keepdims=True)
    flat_e = top_i.reshape(-1).astype(jnp.int32)
    n_slots = T * TOP_K
    order = jnp.argsort(flat_e, stable=True).astype(jnp.int32)
    counts = jnp.sum(flat_e[:, None] == jnp.arange(E, dtype=jnp.int32)[None, :], axis=0).astype(jnp.int32)
    tiles_per = (counts + tm - 1) // tm
    tile_end = jnp.cumsum(tiles_per)
    row_start = (tile_end - tiles_per) * tm
    slot_start = jnp.cumsum(counts) - counts
    n_tiles = n_slots // tm + E
    P = n_tiles * tm
    n_active = tile_end[-1:].astype(jnp.int32)
    tile_ids = jnp.arange(n_tiles, dtype=jnp.int32)
    tile_expert = jnp.sum(tile_ids[:, None] >= tile_end[None, :], axis=1).astype(jnp.int32)
    tile_expert = jnp.minimum(tile_expert, tile_expert[jnp.maximum(n_active[0] - 1, 0)])
    row_e = jnp.repeat(tile_expert, tm)
    rank = jnp.arange(P, dtype=jnp.int32) - row_start[row_e]
    row_ok = jnp.logical_and(jnp.repeat(tile_ids, tm) < n_active[0], rank < counts[row_e])
    row_slot = order[jnp.clip(slot_start[row_e] + rank, 0, n_slots - 1)]
    src_token = jnp.where(row_ok, row_slot // TOP_K, 0)
    row_weight = jnp.where(row_ok, top_p.reshape(-1)[row_slot], 0.0)
    sorted_pos = jnp.argsort(order).astype(jnp.int32)
    dest = row_start[flat_e] + sorted_pos - slot_start[flat_e]
    hs = jnp.take(h, src_token, axis=0)
    y = _ffn(hs, w_gate.astype(BF16), w_up.astype(BF16), w_down.astype(BF16),
             tile_expert, n_active, row_weight.reshape(P, 1), tm)
    dest = dest.reshape(T, TOP_K)
    return jnp.take(y, dest[:, 0], axis=0), jnp.take(y, dest[:, 1], axis=0)


def kernel(x, attn_norm_g, w_in, na_q_g, na_k_g, na_rpb, mla_q_norm_g, mla_kv_norm_g, mla_w_uq, mla_w_ukv, mla_q_g, mla_k_g, hy_short_w, hy_short_b, hy_w1, hy_b1, hy_w2, hy_b2, hy_w3, hy_freq, hy_bias, group_norm_g, w_out, ffn_norm_g, dense_w_gate, dense_w_up, dense_w_down, router_w, moe_w_gate, moe_w_up, moe_w_down):
    B, L, D = x.shape
    T = B * L
    depth = attn_norm_g.shape[0]
    assert L % (NA_K_ROWS * GRID_W) == 0 and w_in.shape[2] == IN_WIDTH
    x2 = x.reshape(T, D).astype(F32)
    dft = _dft_tables(L)
    deltas = []
    for l in range(depth):
        x2, h = _add_norm(x2, deltas, attn_norm_g[l].astype(F32))
        w = w_in[l]
        w_in_p = jnp.concatenate(
            [w[:, :SRC_CKV], w[:, SRC_HY:], w[:, SRC_CKV:SRC_HY],
             jnp.zeros((D, IN_WIDTH_PAD - IN_WIDTH), w.dtype)], axis=1).astype(BF16)
        proj = _matmul(h, w_in_p, BF16, 1024)
        o_na = _neighborhood_attention(proj.reshape(B, L, IN_WIDTH_PAD), na_rpb[l],
                                       na_q_g[l], na_k_g[l]).reshape(T, NA_WIDTH)
        qn, qr, kn, kr, v = _mla_prep(proj, L, mla_q_norm_g[l], mla_kv_norm_g[l], mla_w_uq[l],
                                      mla_w_ukv[l], mla_q_g[l], mla_k_g[l])
        o_mla = _flash_attention(qn, qr, kn, kr, v, B, L).reshape(T, MLA_HEADS * MLA_V_DIM)
        o_hy = _hyena(proj, B, L, hy_short_w[l], hy_short_b[l], hy_w1[l], hy_b1[l], hy_w2[l],
                      hy_b2[l], hy_w3[l], hy_freq[l], hy_bias[l], dft).reshape(T, HY_CH)
        mix = _mix_norm(o_na, o_mla, o_hy, group_norm_g[l])
        d_mix = _matmul(mix, w_out[l].astype(BF16), BF16, 1024)
        i = l // 2
        if l % 2 == 0:
            x2, h = _add_norm(x2, [d_mix], ffn_norm_g[l].astype(F32))
            deltas = [_dense_ffn(h, dense_w_gate[i], dense_w_up[i], dense_w_down[i])]
        else:
            wr = jnp.pad(router_w[i].astype(F32), ((0, 0), (0, LANES - N_EXPERTS)))
            x2, h, logits = _add_norm(x2, [d_mix], ffn_norm_g[l].astype(F32), wr)
            deltas = list(_moe_ffn(h, logits, moe_w_gate[i], moe_w_up[i], moe_w_down[i]))
    (x2,) = _add_norm(x2, deltas)
    return x2.reshape(B, L, D).astype(x.dtype)
```

```python
import functools
import math

import jax
import jax.numpy as jnp
import numpy as np
from jax import lax
from jax.experimental import pallas as pl
from jax.experimental.pallas import tpu as pltpu

F32 = jnp.float32
BF16 = jnp.bfloat16

GRID_W = 64
NA_HEADS = 8
NA_HEAD_DIM = 64
NA_WIDTH = NA_HEADS * NA_HEAD_DIM
NA_WIN_ROWS = 8
NA_WIN_COLS = 16
MLA_HEADS = 8
MLA_NOPE_DIM = 128
MLA_ROPE_DIM = 64
MLA_V_DIM = 128
MLA_QK_DIM = MLA_NOPE_DIM + MLA_ROPE_DIM
MLA_Q_RANK = 512
MLA_KV_RANK = 256
ROPE_THETA = 10000.0
HY_CH = 512
HY_ORDER = 2
HY_POS_BANDS = 16
HY_DECAY_TARGET = 1e-2
HY_DECAY_FAST = 0.3
HY_DECAY_SLOW = 1.5
HY_MAX_DECAY = math.log(HY_DECAY_TARGET) / HY_DECAY_FAST
HY_MIN_DECAY = math.log(HY_DECAY_TARGET) / HY_DECAY_SLOW
N_EXPERTS = 8
TOP_K = 2
EPS = 1e-6
NEG_INF = -1e30

SRC_KROPE = 3 * NA_WIDTH + MLA_Q_RANK + MLA_KV_RANK
SRC_HY = SRC_KROPE + MLA_ROPE_DIM
IN_WIDTH = SRC_HY + (HY_ORDER + 1) * HY_CH
SRC_CKV = 3 * NA_WIDTH + MLA_Q_RANK
COL_NA_Q = 0
COL_NA_K = NA_WIDTH
COL_NA_V = 2 * NA_WIDTH
COL_CQ = 3 * NA_WIDTH
COL_HY = COL_CQ + MLA_Q_RANK
COL_CKV = COL_HY + (HY_ORDER + 1) * HY_CH
COL_KROPE = COL_CKV + MLA_KV_RANK
LANES = 128
IN_WIDTH_PAD = 4096

VMEM_LIMIT_BYTES = 56 * 1024 * 1024

NA_Q_ROWS = 8
NA_K_ROWS = 16


def _params(*sem):
    return pltpu.CompilerParams(dimension_semantics=sem, vmem_limit_bytes=VMEM_LIMIT_BYTES)


def _add_norm_kernel(*refs, n_delta, with_router, with_norm):
    x_ref = refs[0]
    d_refs = refs[1:1 + n_delta]
    pos = 1 + n_delta
    x = x_ref[...]
    for d in d_refs:
        x = x + d[...].astype(F32)
    if not with_norm:
        refs[pos][...] = x
        return
    g_ref = refs[pos]
    pos += 1
    if with_router:
        wr_ref = refs[pos]
        pos += 1
    xo_ref, h_ref = refs[pos], refs[pos + 1]
    xo_ref[...] = x
    h = x * lax.rsqrt(jnp.mean(x * x, axis=-1, keepdims=True) + EPS) * g_ref[...]
    h_ref[...] = h.astype(BF16)
    if with_router:
        refs[pos + 2][...] = jnp.dot(h, wr_ref[...], preferred_element_type=F32,
                                     precision=lax.Precision.HIGHEST)


def _add_norm(x, deltas, gain=None, router_w=None):
    T, D = x.shape
    tm = min(512, T)
    with_norm = gain is not None
    with_router = router_w is not None
    row = pl.BlockSpec((tm, D), lambda i: (i, 0))
    in_specs = [row] + [row] * len(deltas)
    args = [x] + list(deltas)
    out_shape = [jax.ShapeDtypeStruct((T, D), F32)]
    out_specs = [row]
    if with_norm:
        in_specs.append(pl.BlockSpec((1, D), lambda i: (0, 0)))
        args.append(gain.reshape(1, D))
        if with_router:
            in_specs.append(pl.BlockSpec((D, LANES), lambda i: (0, 0)))
            args.append(router_w)
        out_shape.append(jax.ShapeDtypeStruct((T, D), BF16))
        out_specs.append(row)
        if with_router:
            out_shape.append(jax.ShapeDtypeStruct((T, LANES), F32))
            out_specs.append(pl.BlockSpec((tm, LANES), lambda i: (i, 0)))
    out = pl.pallas_call(
        functools.partial(_add_norm_kernel, n_delta=len(deltas), with_router=with_router,
                          with_norm=with_norm),
        grid=(T // tm,), in_specs=in_specs, out_specs=out_specs, out_shape=out_shape,
        compiler_params=_params("parallel"), name="add_norm")(*args)
    return out


def _mm_kernel(a_ref, b_ref, o_ref):
    o_ref[...] = jnp.dot(a_ref[...], b_ref[...], preferred_element_type=F32).astype(o_ref.dtype)


def _matmul(a, b, out_dtype, tn):
    M, K = a.shape
    N = b.shape[1]
    tm = min(1024, M)
    return pl.pallas_call(
        _mm_kernel, grid=(M // tm, N // tn),
        in_specs=[pl.BlockSpec((tm, K), lambda i, j: (i, 0)),
                  pl.BlockSpec((K, tn), lambda i, j: (0, j))],
        out_specs=pl.BlockSpec((tm, tn), lambda i, j: (i, j)),
        out_shape=jax.ShapeDtypeStruct((M, N), out_dtype),
        compiler_params=_params("parallel", "parallel"), name="matmul")(a, b)


def _na_bias_table(rpb, rows):
    n_groups = rows // NA_Q_ROWS
    reps = (0, min(1, n_groups - 1), n_groups - 1)
    row_off = np.zeros((3, NA_Q_ROWS, NA_K_ROWS), np.int32)
    row_ok = np.zeros((3, NA_Q_ROWS, NA_K_ROWS), bool)
    for p, g in enumerate(reps):
        start = int(np.clip(g * NA_Q_ROWS - NA_WIN_ROWS // 2, 0, rows - NA_K_ROWS))
        for i in range(NA_Q_ROWS):
            r = g * NA_Q_ROWS + i
            rs = int(np.clip(r - NA_WIN_ROWS // 2, 0, rows - NA_WIN_ROWS))
            for j in range(NA_K_ROWS):
                kr = start + j
                ok = rs <= kr < rs + NA_WIN_ROWS
                row_ok[p, i, j] = ok
                row_off[p, i, j] = np.clip(kr - r + NA_WIN_ROWS - 1, 0, 2 * NA_WIN_ROWS - 2)
    cols = np.arange(GRID_W)
    col_start = np.clip(cols - NA_WIN_COLS // 2, 0, GRID_W - NA_WIN_COLS)
    col_ok = (cols[None, :] >= col_start[:, None]) & (cols[None, :] < col_start[:, None] + NA_WIN_COLS)
    col_off = np.clip(cols[None, :] - cols[:, None] + NA_WIN_COLS - 1, 0, 2 * NA_WIN_COLS - 2)
    row_sel = np.eye(2 * NA_WIN_ROWS - 1, dtype=np.float32)[row_off]
    col_sel = np.eye(2 * NA_WIN_COLS - 1, dtype=np.float32)[col_off]
    tab = jnp.einsum('hrc,pijr,abc->hpiajb', rpb.astype(F32), row_sel, col_sel,
                     precision=lax.Precision.HIGHEST)
    ok = row_ok[:, :, None, :, None] & col_ok[None, None, :, None, :]
    tab = jnp.where(ok[None], tab, NEG_INF).astype(BF16)
    return tab.reshape(NA_HEADS, 3, NA_Q_ROWS * GRID_W, NA_K_ROWS * GRID_W)


def _na_kernel(q_ref, k_ref, v_ref, bias_ref, gq_ref, gk_ref, ones_ref, o_ref, *, rows):
    g = pl.program_id(2)
    tq = NA_Q_ROWS * GRID_W
    tk = NA_K_ROWS * GRID_W
    start = jnp.clip(g * NA_Q_ROWS - NA_WIN_ROWS // 2, 0, rows - NA_K_ROWS) * GRID_W
    start = pl.multiple_of(start, GRID_W)
    ones_bd = ones_ref[...]

    def head_norm(x, gain):
        x2 = x * x
        hi = x2.astype(BF16)
        lo = (x2 - hi.astype(F32)).astype(BF16)
        ssq = (jnp.dot(hi, ones_bd, preferred_element_type=F32)
               + jnp.dot(lo, ones_bd, preferred_element_type=F32))
        return x * lax.rsqrt(ssq * (1.0 / NA_HEAD_DIM) + EPS) * gain

    q = head_norm(q_ref[0].astype(F32), gq_ref[...]) * (NA_HEAD_DIM ** -0.5)
    kw = head_norm(k_ref[0, pl.ds(start, tk), :].astype(F32), gk_ref[...]).astype(BF16)
    vw = v_ref[0, pl.ds(start, tk), :]
    lane = lax.broadcasted_iota(jnp.int32, (tq, LANES), 1)
    outs = []
    for hh in range(2):
        sel = (lane < NA_HEAD_DIM) if hh == 0 else (lane >= NA_HEAD_DIM)
        qm = jnp.where(sel, q, 0.0).astype(BF16)
        s = lax.dot_general(qm, kw, (((1,), (1,)), ((), ())), preferred_element_type=F32)
        s = s + bias_ref[hh, 0].astype(F32)
        m = jnp.max(s, axis=-1, keepdims=True)
        p = jnp.exp(s - m)
        l = jnp.sum(p, axis=-1, keepdims=True)
        o = jnp.dot(p.astype(BF16), vw, preferred_element_type=F32)
        outs.append(o / l)
    o_ref[0] = jnp.where(lane < NA_HEAD_DIM, outs[0], outs[1]).astype(o_ref.dtype)


def _neighborhood_attention(proj3, rpb, q_g, k_g):
    B, L, _ = proj3.shape
    rows = L // GRID_W
    n_groups = rows // NA_Q_ROWS
    tq = NA_Q_ROWS * GRID_W
    tk = NA_K_ROWS * GRID_W
    bias = _na_bias_table(rpb, rows)
    gq = jnp.tile(q_g.astype(F32), 2).reshape(1, LANES)
    gk = jnp.tile(k_g.astype(F32), 2).reshape(1, LANES)
    ones_bd = jnp.asarray(np.kron(np.eye(2), np.ones((NA_HEAD_DIM, NA_HEAD_DIM))), BF16)
    qb, kb, vb = COL_NA_Q // LANES, COL_NA_K // LANES, COL_NA_V // LANES

    def pattern(g):
        return (g > 0).astype(jnp.int32) + (g == n_groups - 1).astype(jnp.int32)

    return pl.pallas_call(
        functools.partial(_na_kernel, rows=rows),
        grid=(NA_HEADS // 2, B, n_groups),
        in_specs=[
            pl.BlockSpec((1, tq, LANES), lambda hp, b, g: (b, g, qb + hp)),
            pl.BlockSpec((1, L, LANES), lambda hp, b, g: (b, 0, kb + hp)),
            pl.BlockSpec((1, L, LANES), lambda hp, b, g: (b, 0, vb + hp)),
            pl.BlockSpec((2, 1, tq, tk), lambda hp, b, g: (hp, pattern(g), 0, 0)),
            pl.BlockSpec((1, LANES), lambda hp, b, g: (0, 0)),
            pl.BlockSpec((1, LANES), lambda hp, b, g: (0, 0)),
            pl.BlockSpec((LANES, LANES), lambda hp, b, g: (0, 0)),
        ],
        out_specs=pl.BlockSpec((1, tq, LANES), lambda hp, b, g: (b, g, hp)),
        out_shape=jax.ShapeDtypeStruct((B, L, NA_WIDTH), BF16),
        compiler_params=_params("parallel", "parallel", "arbitrary"),
        name="neighborhood_attention")(proj3, proj3, proj3, bias, gq, gk, ones_bd)


def _mla_prep_kernel(cq_ref, ckv_ref, krope_ref, gqn_ref, gkvn_ref, wuq_ref, wukv_ref,
                     qgn_ref, qgr_ref, kgn_ref, kgr_ref, cos_ref, sin_ref,
                     qn_ref, qr_ref, kn_ref, kr_ref, v_ref):
    def rms(x, g):
        return x * lax.rsqrt(jnp.mean(x * x, axis=-1, keepdims=True) + EPS) * g

    tm = cq_ref.shape[0]
    nope_w = MLA_HEADS * MLA_NOPE_DIM
    cq = rms(cq_ref[...].astype(F32), gqn_ref[...]).astype(BF16)
    q = jnp.dot(cq, wuq_ref[...], preferred_element_type=F32)
    ckv = rms(ckv_ref[...].astype(F32), gkvn_ref[...]).astype(BF16)
    kv = jnp.dot(ckv, wukv_ref[...], preferred_element_type=F32)
    v_ref[...] = kv[:, nope_w:].astype(BF16)

    lane = lax.broadcasted_iota(jnp.int32, (tm, LANES), 1)
    first = lane < MLA_ROPE_DIM
    low_half = (lane % MLA_ROPE_DIM) < (MLA_ROPE_DIM // 2)
    cosv = cos_ref[...]
    sinv = sin_ref[...]
    scale = (MLA_QK_DIM ** -0.5) * math.log2(math.e)
    inv_dim = 1.0 / MLA_QK_DIM

    def rope(x):
        swapped = jnp.where(low_half, pltpu.roll(x, LANES - MLA_ROPE_DIM // 2, 1),
                            pltpu.roll(x, MLA_ROPE_DIM // 2, 1))
        return x * cosv + swapped * sinv

    qgn, qgr = qgn_ref[...], qgr_ref[...]
    kgn, kgr = kgn_ref[...], kgr_ref[...]
    kraw = krope_ref[...].astype(F32)
    kdup = jnp.where(first, kraw, pltpu.roll(kraw, MLA_ROPE_DIM, 1))
    k_rope_sq = jnp.where(first, kraw * kraw, 0.0)
    k_roped = rope(kdup * kgr)

    for j in range(MLA_HEADS // 2):
        qr_j = q[:, nope_w + LANES * j:nope_w + LANES * (j + 1)]
        qr_sq = qr_j * qr_j
        q_rs, k_rs = [], []
        for hh in range(2):
            h = 2 * j + hh
            cols = slice(LANES * h, LANES * (h + 1))
            qn_h = q[:, cols]
            t = qn_h * qn_h + jnp.where(first if hh == 0 else jnp.logical_not(first), qr_sq, 0.0)
            rs = lax.rsqrt(jnp.sum(t, axis=-1, keepdims=True) * inv_dim + EPS)
            qn_ref[:, cols] = (qn_h * rs * qgn * scale).astype(BF16)
            q_rs.append(rs)
            kn_h = kv[:, cols]
            t = kn_h * kn_h + k_rope_sq
            rs = lax.rsqrt(jnp.sum(t, axis=-1, keepdims=True) * inv_dim + EPS)
            kn_ref[:, cols] = (kn_h * rs * kgn).astype(BF16)
            k_rs.append(rs)
        pair = slice(LANES * j, LANES * (j + 1))
        qr_ref[:, pair] = (rope(qr_j * jnp.where(first, q_rs[0], q_rs[1]) * qgr) * scale).astype(BF16)
        kr_ref[:, pair] = (k_roped * jnp.where(first, k_rs[0], k_rs[1])).astype(BF16)


def _mla_prep(proj, L, q_norm_g, kv_norm_g, w_uq, w_ukv, q_head_g, k_head_g):
    T = proj.shape[0]
    tm = min(512, L)
    H = MLA_HEADS
    wq = w_uq.reshape(MLA_Q_RANK, H, MLA_QK_DIM)
    wq = jnp.concatenate([wq[:, :, :MLA_NOPE_DIM].reshape(MLA_Q_RANK, -1),
                          wq[:, :, MLA_NOPE_DIM:].reshape(MLA_Q_RANK, -1)], axis=1).astype(BF16)
    wkv = w_ukv.reshape(MLA_KV_RANK, H, MLA_NOPE_DIM + MLA_V_DIM)
    wkv = jnp.concatenate([wkv[:, :, :MLA_NOPE_DIM].reshape(MLA_KV_RANK, -1),
                           wkv[:, :, MLA_NOPE_DIM:].reshape(MLA_KV_RANK, -1)], axis=1).astype(BF16)
    half = MLA_ROPE_DIM // 2
    freqs = ROPE_THETA ** (-jnp.arange(half, dtype=F32) / half)
    ang = jnp.arange(L).astype(F32)[:, None] * freqs[None, :]
    cos, sin = jnp.cos(ang), jnp.sin(ang)
    cos2 = jnp.tile(jnp.concatenate([cos, cos], axis=1), (1, 2))
    sin2 = jnp.tile(jnp.concatenate([-sin, sin], axis=1), (1, 2))

    def vec(g, reps=1):
        return jnp.tile(g.astype(F32), reps).reshape(1, -1)

    const = lambda shape: pl.BlockSpec(shape, lambda i: (0, 0))
    nblk = L // tm
    outs = pl.pallas_call(
        _mla_prep_kernel, grid=(T // tm,),
        in_specs=[
            pl.BlockSpec((tm, MLA_Q_RANK), lambda i: (i, COL_CQ // MLA_Q_RANK)),
            pl.BlockSpec((tm, MLA_KV_RANK), lambda i: (i, COL_CKV // MLA_KV_RANK)),
            pl.BlockSpec((tm, LANES), lambda i: (i, COL_KROPE // LANES)),
            const((1, MLA_Q_RANK)), const((1, MLA_KV_RANK)),
            const(wq.shape), const(wkv.shape),
            const((1, LANES)), const((1, LANES)), const((1, LANES)), const((1, LANES)),
            pl.BlockSpec((tm, LANES), lambda i: (i % nblk, 0)),
            pl.BlockSpec((tm, LANES), lambda i: (i % nblk, 0)),
        ],
        out_specs=[
            pl.BlockSpec((tm, H * MLA_NOPE_DIM), lambda i: (i, 0)),
            pl.BlockSpec((tm, H * MLA_ROPE_DIM), lambda i: (i, 0)),
            pl.BlockSpec((tm, H * MLA_NOPE_DIM), lambda i: (i, 0)),
            pl.BlockSpec((tm, H * MLA_ROPE_DIM), lambda i: (i, 0)),
            pl.BlockSpec((tm, H * MLA_V_DIM), lambda i: (i, 0)),
        ],
        out_shape=[
            jax.ShapeDtypeStruct((T, H * MLA_NOPE_DIM), BF16),
            jax.ShapeDtypeStruct((T, H * MLA_ROPE_DIM), BF16),
            jax.ShapeDtypeStruct((T, H * MLA_NOPE_DIM), BF16),
            jax.ShapeDtypeStruct((T, H * MLA_ROPE_DIM), BF16),
            jax.ShapeDtypeStruct((T, H * MLA_V_DIM), BF16),
        ],
        compiler_params=_params("parallel"), name="mla_prep")(
            proj, proj, proj, vec(q_norm_g), vec(kv_norm_g), wq, wkv,
            vec(q_head_g[:MLA_NOPE_DIM]), vec(q_head_g[MLA_NOPE_DIM:], 2),
            vec(k_head_g[:MLA_NOPE_DIM]), vec(k_head_g[MLA_NOPE_DIM:], 2), cos2, sin2)
    return outs


FLASH_STREAM_ROWS = 256


def _flash_kernel(qn_ref, qr_ref, kn0_ref, kr0_ref, kn1_ref, kr1_ref, v_ref, o_ref,
                  q_sc, s_even, s_odd, m_sc, acc_sc):
    h = pl.program_id(1)
    ki = pl.program_id(3)
    tq = qn_ref.shape[1]
    tk = kn0_ref.shape[1]
    rows = min(FLASH_STREAM_ROWS, tq)
    n_streams = tq // rows
    nt = (((1,), (1,)), ((), ()))

    @pl.when(ki == 0)
    def _():
        lane = lax.broadcasted_iota(jnp.int32, (tq, LANES), 1)
        own = (lane // MLA_ROPE_DIM) == (h % 2)
        q_sc[:, :LANES] = qn_ref[0]
        q_sc[:, LANES:] = jnp.where(own, qr_ref[0], jnp.zeros_like(qr_ref[0]))
        m_sc[...] = jnp.full_like(m_sc, NEG_INF)
        acc_sc[...] = jnp.zeros_like(acc_sc)
        k0 = jnp.concatenate([kn0_ref[0], kr0_ref[0]], axis=-1)
        for r in range(n_streams):
            sl = slice(r * rows, (r + 1) * rows)
            s_even[sl, :] = lax.dot_general(q_sc[sl, :], k0, nt, preferred_element_type=F32)

    def step(s_cur, s_next):
        k1 = jnp.concatenate([kn1_ref[0], kr1_ref[0]], axis=-1)
        v_ones = jnp.concatenate([v_ref[0], jnp.ones((tk, LANES), BF16)], axis=-1)
        for r in range(n_streams):
            sl = slice(r * rows, (r + 1) * rows)
            s_next[sl, :] = lax.dot_general(q_sc[sl, :], k1, nt, preferred_element_type=F32)
            s = s_cur[sl, :]
            m_prev = m_sc[sl, :]
            m_new = jnp.maximum(m_prev, jnp.max(s, axis=-1, keepdims=True))
            p = jnp.exp2(s - m_new).astype(BF16)
            acc_sc[sl, :] = (jnp.exp2(m_prev - m_new) * acc_sc[sl, :]
                             + jnp.dot(p, v_ones, preferred_element_type=F32))
            m_sc[sl, :] = m_new

    @pl.when(ki % 2 == 0)
    def _():
        step(s_even, s_odd)

    @pl.when(ki % 2 == 1)
    def _():
        step(s_odd, s_even)

    @pl.when(ki == pl.num_programs(3) - 1)
    def _():
        o_ref[0] = (acc_sc[:, :LANES] / acc_sc[:, LANES:]).astype(o_ref.dtype)


def _flash_attention(qn, qr, kn, kr, v, B, L):
    H = MLA_HEADS
    tq = min(2048, L)
    tk = min(1024, L)
    r3 = lambda a: a.reshape(B, L, a.shape[-1])
    nk = L // tk
    ahead = lambda ki: jnp.minimum(ki + 1, nk - 1)
    return pl.pallas_call(
        _flash_kernel, grid=(B, H, L // tq, nk),
        in_specs=[
            pl.BlockSpec((1, tq, LANES), lambda b, h, qi, ki: (b, qi, h)),
            pl.BlockSpec((1, tq, LANES), lambda b, h, qi, ki: (b, qi, h // 2)),
            pl.BlockSpec((1, tk, LANES), lambda b, h, qi, ki: (b, 0, h)),
            pl.BlockSpec((1, tk, LANES), lambda b, h, qi, ki: (b, 0, h // 2)),
            pl.BlockSpec((1, tk, LANES), lambda b, h, qi, ki: (b, ahead(ki), h)),
            pl.BlockSpec((1, tk, LANES), lambda b, h, qi, ki: (b, ahead(ki), h // 2)),
            pl.BlockSpec((1, tk, LANES), lambda b, h, qi, ki: (b, ki, h)),
        ],
        out_specs=pl.BlockSpec((1, tq, LANES), lambda b, h, qi, ki: (b, qi, h)),
        out_shape=jax.ShapeDtypeStruct((B, L, H * MLA_V_DIM), BF16),
        scratch_shapes=[pltpu.VMEM((tq, 2 * LANES), BF16), pltpu.VMEM((tq, tk), F32),
                        pltpu.VMEM((tq, tk), F32), pltpu.VMEM((tq, 1), F32),
                        pltpu.VMEM((tq, 2 * LANES), F32)],
        compiler_params=_params("parallel", "parallel", "parallel", "arbitrary"),
        name="mla_flash")(r3(qn), r3(qr), r3(kn), r3(kr), r3(kn), r3(kr), r3(v))


def _hyena_filters(L, w1, b1, w2, b2, w3, freq):
    t = jnp.linspace(0.0, 1.0, L, dtype=F32)[:, None]
    bands = jnp.linspace(1e-4, HY_POS_BANDS - 1, HY_POS_BANDS, dtype=F32)[None, :]
    w = 2.0 * math.pi * jnp.arange(L, dtype=F32)[:, None] / L
    z = jnp.concatenate([t, jnp.cos(bands * w), -jnp.sin(bands * w)], axis=-1)
    hp = lax.Precision.HIGHEST
    deltas = jnp.abs(jnp.linspace(HY_MIN_DECAY, HY_MAX_DECAY, HY_CH, dtype=F32))
    w3d = w3.astype(F32).reshape(w3.shape[0], 2, HY_ORDER * HY_CH)

    def direction(z, t, d):
        a = jnp.sin(freq[0] * (jnp.matmul(z, w1, precision=hp) + b1))
        a = jnp.sin(freq[1] * (jnp.matmul(a, w2, precision=hp) + b2))
        f = jnp.matmul(a, w3d[:, d], precision=hp).reshape(L, HY_ORDER, HY_CH)
        return f * jnp.exp(-t * deltas[None, :])[:, None, :]

    fwd = direction(z, t, 0)
    bwd_rev = direction(z[::-1], t[::-1], 1)[:L - 1]
    k = jnp.concatenate([fwd, jnp.zeros((1, HY_ORDER, HY_CH), F32), bwd_rev], axis=0)
    return k * lax.rsqrt(jnp.sum(k * k, axis=0, keepdims=True) + EPS)


HALO_ROWS = 16


def _short_conv_kernel(x_ref, xp_ref, xn_ref, w_ref, b_ref, o_ref, *, tiles_per_seq):
    i = pl.program_id(0)
    x = x_ref[...].astype(F32)
    tm = x.shape[0]
    row = lax.broadcasted_iota(jnp.int32, x.shape, 0)
    t = i % tiles_per_seq
    halo_prev = jnp.where(t == 0, 0.0, xp_ref[HALO_ROWS - 1:HALO_ROWS, :].astype(F32))
    halo_next = jnp.where(t == tiles_per_seq - 1, 0.0, xn_ref[0:1, :].astype(F32))
    prev = jnp.where(row == 0, halo_prev, pltpu.roll(x, 1, 0))
    nxt = jnp.where(row == tm - 1, halo_next, pltpu.roll(x, tm - 1, 0))
    o_ref[...] = prev * w_ref[0:1, :] + x * w_ref[1:2, :] + nxt * w_ref[2:3, :] + b_ref[...]


def _short_conv(proj, L, w, b):
    T = proj.shape[0]
    W = (HY_ORDER + 1) * HY_CH
    tm = min(1024, L)
    tc = 512
    c0 = COL_HY // tc
    hb = tm // HALO_ROWS
    n_halo = T // HALO_ROWS
    return pl.pallas_call(
        functools.partial(_short_conv_kernel, tiles_per_seq=L // tm),
        grid=(T // tm, W // tc),
        in_specs=[
            pl.BlockSpec((tm, tc), lambda i, c: (i, c0 + c)),
            pl.BlockSpec((HALO_ROWS, tc), lambda i, c: (jnp.maximum(i * hb - 1, 0), c0 + c)),
            pl.BlockSpec((HALO_ROWS, tc), lambda i, c: (jnp.minimum((i + 1) * hb, n_halo - 1), c0 + c)),
            pl.BlockSpec((3, tc), lambda i, c: (0, c)),
            pl.BlockSpec((1, tc), lambda i, c: (0, c)),
        ],
        out_specs=pl.BlockSpec((tm, tc), lambda i, c: (i, c)),
        out_shape=jax.ShapeDtypeStruct((T, W), F32),
        compiler_params=_params("parallel", "parallel"), name="hyena_short_conv")(
            proj, proj, proj, w.astype(F32), b.astype(F32).reshape(1, W))


HY_CB = LANES
HY_NLO = 128
HY_UNROLL = 4
SUBLANES = 8


def _dft_geometry(L):
    nhi = 2 * L // HY_NLO
    n_kb = nhi // 2 + 1
    kb_pad = -(-n_kb // SUBLANES) * SUBLANES
    chunk = max(d for d in range(1, 17) if n_kb % d == 0)
    return nhi, n_kb, kb_pad, chunk


def _dft_tables(L):
    N = 2 * L
    nhi, n_kb, kb_pad, _ = _dft_geometry(L)
    kb = jnp.arange(kb_pad, dtype=jnp.int32)
    nh = jnp.arange(nhi, dtype=jnp.int32)
    ang = (2.0 * math.pi / nhi) * ((kb[:, None] * nh[None, :]) % nhi).astype(F32)
    cos, sin = jnp.cos(ang), jnp.sin(ang)
    f1 = jnp.concatenate([cos, -sin], axis=0)
    wgt = jnp.where((kb == 0) | (kb == nhi // 2), 1.0, jnp.where(kb < n_kb, 2.0, 0.0))[:, None]
    f1_inv = jnp.concatenate([cos * wgt, -sin * wgt], axis=0)[:, :nhi // 2].T
    ka = jnp.arange(HY_NLO, dtype=jnp.int32)
    idx = (ka[None, None, :] * (ka[None, :, None] * nhi + kb[:n_kb, None, None])) % N
    ang = (2.0 * math.pi / N) * idx.astype(F32)
    gr, gi = jnp.cos(ang), -jnp.sin(ang)
    g = jnp.concatenate([jnp.concatenate([gr, -gi], axis=2),
                         jnp.concatenate([gi, gr], axis=2)], axis=1)
    return f1.astype(BF16), f1_inv.astype(BF16), g.astype(BF16)


def _dft_major_stage(src_ref, f1, a_ref, n_rows):
    two_nhi = f1.shape[0]

    def body(n_lo, carry):
        xs = src_ref[pl.ds(n_lo, n_rows, stride=HY_NLO), :]
        dst = pl.multiple_of(n_lo * two_nhi, two_nhi)
        a_ref[pl.ds(dst, two_nhi), :] = jnp.dot(f1, xs.astype(BF16), preferred_element_type=F32)
        return carry

    lax.fori_loop(0, HY_NLO, body, 0, unroll=HY_UNROLL)


def _load_slab(a_ref, kb, kb_pad):
    re = a_ref[pl.ds(kb, HY_NLO, stride=2 * kb_pad), :]
    im = a_ref[pl.ds(kb_pad + kb, HY_NLO, stride=2 * kb_pad), :]
    return jnp.concatenate([re, im], axis=0).astype(BF16)


def _hyena_spectrum_kernel(k_ref, f1_ref, g_ref, o_ref, a_ref, *, nhi, kb_pad, chunk):
    j = pl.program_id(1)

    @pl.when(j == 0)
    def _():
        _dft_major_stage(k_ref, f1_ref[...], a_ref, nhi)

    def slab(s, carry):
        kb = j * chunk + s
        o_ref[0, s] = jnp.dot(g_ref[s], _load_slab(a_ref, kb, kb_pad), preferred_element_type=F32)
        return carry

    lax.fori_loop(0, chunk, slab, 0, unroll=min(HY_UNROLL, chunk))


def _hyena_spectrum(k2, f1, g):
    N, n_ch = k2.shape
    nhi, n_kb, kb_pad, chunk = _dft_geometry(N // 2)
    return pl.pallas_call(
        functools.partial(_hyena_spectrum_kernel, nhi=nhi, kb_pad=kb_pad, chunk=chunk),
        grid=(n_ch // HY_CB, n_kb // chunk),
        in_specs=[
            pl.BlockSpec((N, HY_CB), lambda c, j: (0, c)),
            pl.BlockSpec(f1.shape, lambda c, j: (0, 0)),
            pl.BlockSpec((chunk, 2 * HY_NLO, 2 * HY_NLO), lambda c, j: (j, 0, 0)),
        ],
        out_specs=pl.BlockSpec((1, chunk, 2 * HY_NLO, HY_CB), lambda c, j: (c, j, 0, 0)),
        out_shape=jax.ShapeDtypeStruct((n_ch // HY_CB, n_kb, 2 * HY_NLO, HY_CB), F32),
        scratch_shapes=[pltpu.VMEM((HY_NLO * 2 * kb_pad, HY_CB), F32)],
        compiler_params=_params("parallel", "arbitrary"), name="hyena_spectrum")(k2, f1, g)


def _hyena_conv_kernel(u_ref, gate_ref, bias_ref, f1_ref, f1t_ref, g_ref, kf_ref, o_ref,
                       a_ref, y_ref, *, n_in, kb_pad, chunk):
    j = pl.program_id(2)
    L = n_in * HY_NLO

    @pl.when(j == 0)
    def _():
        _dft_major_stage(u_ref.at[0], f1_ref[...], a_ref, n_in)

    def forward(s, carry):
        kb = j * chunk + s
        t = jnp.dot(g_ref[s], _load_slab(a_ref, kb, kb_pad), preferred_element_type=F32)
        kf = kf_ref[0, s]
        tr, ti = t[:HY_NLO], t[HY_NLO:]
        kr, ki = kf[:HY_NLO], kf[HY_NLO:]
        y_ref[s] = jnp.concatenate([tr * kr - ti * ki, tr * ki + ti * kr], axis=0).astype(BF16)
        return carry

    lax.fori_loop(0, chunk, forward, 0, unroll=min(HY_UNROLL, chunk))

    def inverse(s, carry):
        kb = j * chunk + s
        r = lax.dot_general(g_ref[s], y_ref[s], (((0,), (0,)), ((), ())),
                            preferred_element_type=F32)
        a_ref[pl.ds(kb, HY_NLO, stride=2 * kb_pad), :] = r[:HY_NLO]
        a_ref[pl.ds(kb_pad + kb, HY_NLO, stride=2 * kb_pad), :] = r[HY_NLO:]
        return carry

    lax.fori_loop(0, chunk, inverse, 0, unroll=min(HY_UNROLL, chunk))

    @pl.when(j == pl.num_programs(2) - 1)
    def _():
        f1t = f1t_ref[...]

        def body(n_lo, carry):
            src = pl.multiple_of(n_lo * 2 * kb_pad, 2 * kb_pad)
            blk = a_ref[pl.ds(src, 2 * kb_pad), :].astype(BF16)
            o_ref[0, pl.ds(n_lo, n_in, stride=HY_NLO), :] = jnp.dot(
                f1t, blk, preferred_element_type=F32)
            return carry

        lax.fori_loop(0, HY_NLO, body, 0, unroll=HY_UNROLL)
        inv_n = 1.0 / (2 * L)
        rows = 512

        def gate_rows(c, carry):
            r0 = pl.multiple_of(c * rows, rows)
            sl = pl.ds(r0, rows)
            o_ref[0, sl, :] = gate_ref[0, sl, :] * (o_ref[0, sl, :] * inv_n
                                                    + u_ref[0, sl, :] * bias_ref[...])
            return carry

        lax.fori_loop(0, L // rows, gate_rows, 0)


def _hyena_conv(u3, u_col, gate3, gate_col, bias, kf, kf_row, f1, f1_inv, g):
    B, L, _ = u3.shape
    nhi, n_kb, kb_pad, chunk = _dft_geometry(L)
    n_in = nhi // 2
    f1_in = f1[:, :n_in]
    n_cb = HY_CH // HY_CB
    return pl.pallas_call(
        functools.partial(_hyena_conv_kernel, n_in=n_in, kb_pad=kb_pad, chunk=chunk),
        grid=(B, n_cb, n_kb // chunk),
        in_specs=[
            pl.BlockSpec((1, L, HY_CB), lambda b, c, j: (b, 0, u_col + c)),
            pl.BlockSpec((1, L, HY_CB), lambda b, c, j: (b, 0, gate_col + c)),
            pl.BlockSpec((1, HY_CB), lambda b, c, j: (0, c)),
            pl.BlockSpec(f1_in.shape, lambda b, c, j: (0, 0)),
            pl.BlockSpec(f1_inv.shape, lambda b, c, j: (0, 0)),
            pl.BlockSpec((chunk, 2 * HY_NLO, 2 * HY_NLO), lambda b, c, j: (j, 0, 0)),
            pl.BlockSpec((1, chunk, 2 * HY_NLO, HY_CB), lambda b, c, j: (kf_row + c, j, 0, 0)),
        ],
        out_specs=pl.BlockSpec((1, L, HY_CB), lambda b, c, j: (b, 0, c)),
        out_shape=jax.ShapeDtypeStruct((B, L, HY_CH), F32),
        scratch_shapes=[pltpu.VMEM((HY_NLO * 2 * kb_pad, HY_CB), F32),
                        pltpu.VMEM((chunk, 2 * HY_NLO, HY_CB), BF16)],
        compiler_params=_params("parallel", "parallel", "arbitrary"), name="hyena_long_conv")(
            u3, gate3, bias.astype(F32).reshape(1, HY_CH), f1_in, f1_inv, g, kf)


def _hyena(proj, B, L, short_w, short_b, w1, b1, w2, b2, w3, freq, bias, dft):
    f1, f1_inv, g = dft
    u = _short_conv(proj, L, short_w, short_b).reshape(B, L, (HY_ORDER + 1) * HY_CH)
    k = _hyena_filters(L, w1, b1, w2, b2, w3, freq).reshape(2 * L, HY_ORDER * HY_CH)
    kf = _hyena_spectrum(k, f1, g)
    n_cb = HY_CH // HY_CB
    z = _hyena_conv(u, 0, u, n_cb, bias[0], kf, 0, f1, f1_inv, g)
    return _hyena_conv(z, 0, u, 2 * n_cb, bias[1], kf, n_cb, f1, f1_inv, g)


def _mix_norm_kernel(na_ref, mla_ref, hy_ref, g_ref, o_ref):
    def rms(x, g):
        return (x * lax.rsqrt(jnp.mean(x * x, axis=-1, keepdims=True) + EPS) * g).astype(BF16)

    a, b = NA_WIDTH, NA_WIDTH + MLA_HEADS * MLA_V_DIM
    o_ref[:, :a] = rms(na_ref[...].astype(F32), g_ref[:, :a])
    o_ref[:, a:b] = rms(mla_ref[...].astype(F32), g_ref[:, a:b])
    o_ref[:, b:] = rms(hy_ref[...].astype(F32), g_ref[:, b:])


def _mix_norm(o_na, o_mla, o_hy, gain):
    T = o_na.shape[0]
    W = o_na.shape[1] + o_mla.shape[1] + o_hy.shape[1]
    tm = min(1024, T)
    row = lambda a: pl.BlockSpec((tm, a.shape[1]), lambda i: (i, 0))
    return pl.pallas_call(
        _mix_norm_kernel, grid=(T // tm,),
        in_specs=[row(o_na), row(o_mla), row(o_hy), pl.BlockSpec((1, W), lambda i: (0, 0))],
        out_specs=pl.BlockSpec((tm, W), lambda i: (i, 0)),
        out_shape=jax.ShapeDtypeStruct((T, W), BF16),
        compiler_params=_params("parallel"), name="mix_norm")(
            o_na, o_mla, o_hy, gain.astype(F32).reshape(1, W))


def _ffn_kernel(te_ref, na_ref, h_ref, wg_ref, wu_ref, wd_ref, rw_ref, o_ref, acc_ref):
    i = pl.program_id(0)
    j = pl.program_id(1)
    last = pl.num_programs(1) - 1
    active = i < na_ref[0]

    @pl.when(jnp.logical_and(active, j == 0))
    def _():
        acc_ref[...] = jnp.zeros_like(acc_ref)

    @pl.when(active)
    def _():
        h = h_ref[...]
        g = jnp.dot(h, wg_ref[0], preferred_element_type=F32)
        u = jnp.dot(h, wu_ref[0], preferred_element_type=F32)
        a = (g * (1.0 / (1.0 + jnp.exp(-g))) * u).astype(BF16)
        acc_ref[...] += jnp.dot(a, wd_ref[0], preferred_element_type=F32)

    @pl.when(jnp.logical_and(active, j == last))
    def _():
        o_ref[...] = (acc_ref[...] * rw_ref[...]).astype(o_ref.dtype)

    @pl.when(jnp.logical_and(jnp.logical_not(active), j == last))
    def _():
        o_ref[...] = jnp.zeros_like(o_ref)


def _ffn(h, w_gate, w_up, w_down, tile_expert, n_active, row_weight, tm):
    P, D = h.shape
    F = w_gate.shape[2]
    tf = 512
    nf = F // tf

    def fidx(i, j, na):
        return jnp.where(i < na[0], j, nf - 1)

    grid_spec = pltpu.PrefetchScalarGridSpec(
        num_scalar_prefetch=2, grid=(P // tm, nf),
        in_specs=[
            pl.BlockSpec((tm, D), lambda i, j, te, na: (i, 0)),
            pl.BlockSpec((1, D, tf), lambda i, j, te, na: (te[i], 0, fidx(i, j, na))),
            pl.BlockSpec((1, D, tf), lambda i, j, te, na: (te[i], 0, fidx(i, j, na))),
            pl.BlockSpec((1, tf, D), lambda i, j, te, na: (te[i], fidx(i, j, na), 0)),
            pl.BlockSpec((tm, 1), lambda i, j, te, na: (i, 0)),
        ],
        out_specs=pl.BlockSpec((tm, D), lambda i, j, te, na: (i, 0)),
        scratch_shapes=[pltpu.VMEM((tm, D), F32)])
    return pl.pallas_call(
        _ffn_kernel, grid_spec=grid_spec,
        out_shape=jax.ShapeDtypeStruct((P, D), BF16),
        compiler_params=_params("parallel", "arbitrary"), name="swiglu_ffn")(
            tile_expert, n_active, h, w_gate, w_up, w_down, row_weight)


def _dense_ffn(h, w_gate, w_up, w_down):
    T = h.shape[0]
    tm = min(1024, T)
    n = T // tm
    return _ffn(h, w_gate[None].astype(BF16), w_up[None].astype(BF16), w_down[None].astype(BF16),
                jnp.zeros((n,), jnp.int32), jnp.full((1,), n, jnp.int32),
                jnp.ones((T, 1), F32), tm)


def _moe_ffn(h, logits, w_gate, w_up, w_down):
    T, D = h.shape
    E = N_EXPERTS
    tm = min(1024, T)
    probs = jax.nn.softmax(logits[:, :E], axis=-1)
    top_p, top_i = lax.top_k(probs, TOP_K)
    top_p = top_p / jnp.sum(top_p, axis=-1, keepdims=True)
    flat_e = top_i.reshape(-1).astype(jnp.int32)
    n_slots = T * TOP_K
    order = jnp.argsort(flat_e, stable=True).astype(jnp.int32)
    counts = jnp.sum(flat_e[:, None] == jnp.arange(E, dtype=jnp.int32)[None, :], axis=0).astype(jnp.int32)
    tiles_per = (counts + tm - 1) // tm
    tile_end = jnp.cumsum(tiles_per)
    row_start = (tile_end - tiles_per) * tm
    slot_start = jnp.cumsum(counts) - counts
    n_tiles = n_slots // tm + E
    P = n_tiles * tm
    n_active = tile_end[-1:].astype(jnp.int32)
    tile_ids = jnp.arange(n_tiles, dtype=jnp.int32)
    tile_expert = jnp.sum(tile_ids[:, None] >= tile_end[None, :], axis=1).astype(jnp.int32)
    tile_expert = jnp.minimum(tile_expert, tile_expert[jnp.maximum(n_active[0] - 1, 0)])
    row_e = jnp.repeat(tile_expert, tm)
    rank = jnp.arange(P, dtype=jnp.int32) - row_start[row_e]
    row_ok = jnp.logical_and(jnp.repeat(tile_ids, tm) < n_active[0], rank < counts[row_e])
    row_slot = order[jnp.clip(slot_start[row_e] + rank, 0, n_slots - 1)]
    src_token = jnp.where(row_ok, row_slot // TOP_K, 0)
    row_weight = jnp.where(row_ok, top_p.reshape(-1)[row_slot], 0.0)
    sorted_pos = jnp.argsort(order).astype(jnp.int32)
    dest = row_start[flat_e] + sorted_pos - slot_start[flat_e]
    hs = jnp.take(h, src_token, axis=0)
    y = _ffn(hs, w_gate.astype(BF16), w_up.astype(BF16), w_down.astype(BF16),
             tile_expert, n_active, row_weight.reshape(P, 1), tm)
    dest = dest.reshape(T, TOP_K)
    return jnp.take(y, dest[:, 0], axis=0), jnp.take(y, dest[:, 1], axis=0)


def kernel(x, attn_norm_g, w_in, na_q_g, na_k_g, na_rpb, mla_q_norm_g, mla_kv_norm_g, mla_w_uq, mla_w_ukv, mla_q_g, mla_k_g, hy_short_w, hy_short_b, hy_w1, hy_b1, hy_w2, hy_b2, hy_w3, hy_freq, hy_bias, group_norm_g, w_out, ffn_norm_g, dense_w_gate, dense_w_up, dense_w_down, router_w, moe_w_gate, moe_w_up, moe_w_down):
    B, L, D = x.shape
    T = B * L
    depth = attn_norm_g.shape[0]
    assert L % (NA_K_ROWS * GRID_W) == 0 and w_in.shape[2] == IN_WIDTH
    x2 = x.reshape(T, D).astype(F32)
    dft = _dft_tables(L)
    deltas = []
    for l in range(depth):
        x2, h = _add_norm(x2, deltas, attn_norm_g[l].astype(F32))
        w = w_in[l]
        w_in_p = jnp.concatenate(
            [w[:, :SRC_CKV], w[:, SRC_HY:], w[:, SRC_CKV:SRC_HY],
             jnp.zeros((D, IN_WIDTH_PAD - IN_WIDTH), w.dtype)], axis=1).astype(BF16)
        proj = _matmul(h, w_in_p, BF16, 1024)
        o_na = _neighborhood_attention(proj.reshape(B, L, IN_WIDTH_PAD), na_rpb[l],
                                       na_q_g[l], na_k_g[l]).reshape(T, NA_WIDTH)
        qn, qr, kn, kr, v = _mla_prep(proj, L, mla_q_norm_g[l], mla_kv_norm_g[l], mla_w_uq[l],
                                      mla_w_ukv[l], mla_q_g[l], mla_k_g[l])
        o_mla = _flash_attention(qn, qr, kn, kr, v, B, L).reshape(T, MLA_HEADS * MLA_V_DIM)
        o_hy = _hyena(proj, B, L, hy_short_w[l], hy_short_b[l], hy_w1[l], hy_b1[l], hy_w2[l],
                      hy_b2[l], hy_w3[l], hy_freq[l], hy_bias[l], dft).reshape(T, HY_CH)
        mix = _mix_norm(o_na, o_mla, o_hy, group_norm_g[l])
        d_mix = _matmul(mix, w_out[l].astype(BF16), BF16, 1024)
        i = l // 2
        if l % 2 == 0:
            x2, h = _add_norm(x2, [d_mix], ffn_norm_g[l].astype(F32))
            deltas = [_dense_ffn(h, dense_w_gate[i], dense_w_up[i], dense_w_down[i])]
        else:
            wr = jnp.pad(router_w[i].astype(F32), ((0, 0), (0, LANES - N_EXPERTS)))
            x2, h, logits = _add_norm(x2, [d_mix], ffn_norm_g[l].astype(F32), wr)
            deltas = list(_moe_ffn(h, logits, moe_w_gate[i], moe_w_up[i], moe_w_down[i]))
    (x2,) = _add_norm(x2, deltas)
    return x2.reshape(B, L, D).astype(x.dtype)
```

```python
import functools
import math

import jax
import jax.numpy as jnp
import numpy as np
from jax import lax
from jax.experimental import pallas as pl
from jax.experimental.pallas import tpu as pltpu

F32 = jnp.float32
BF16 = jnp.bfloat16

GRID_W = 64
NA_HEADS = 8
NA_HEAD_DIM = 64
NA_WIDTH = NA_HEADS * NA_HEAD_DIM
NA_WIN_ROWS = 8
NA_WIN_COLS = 16
MLA_HEADS = 8
MLA_NOPE_DIM = 128
MLA_ROPE_DIM = 64
MLA_V_DIM = 128
MLA_QK_DIM = MLA_NOPE_DIM + MLA_ROPE_DIM
MLA_Q_RANK = 512
MLA_KV_RANK = 256
ROPE_THETA = 10000.0
HY_CH = 512
HY_ORDER = 2
HY_POS_BANDS = 16
HY_DECAY_TARGET = 1e-2
HY_DECAY_FAST = 0.3
HY_DECAY_SLOW = 1.5
HY_MAX_DECAY = math.log(HY_DECAY_TARGET) / HY_DECAY_FAST
HY_MIN_DECAY = math.log(HY_DECAY_TARGET) / HY_DECAY_SLOW
N_EXPERTS = 8
TOP_K = 2
EPS = 1e-6
NEG_INF = -1e30

SRC_KROPE = 3 * NA_WIDTH + MLA_Q_RANK + MLA_KV_RANK
SRC_HY = SRC_KROPE + MLA_ROPE_DIM
IN_WIDTH = SRC_HY + (HY_ORDER + 1) * HY_CH
SRC_CKV = 3 * NA_WIDTH + MLA_Q_RANK
COL_NA_Q = 0
COL_NA_K = NA_WIDTH
COL_NA_V = 2 * NA_WIDTH
COL_CQ = 3 * NA_WIDTH
COL_HY = COL_CQ + MLA_Q_RANK
COL_CKV = COL_HY + (HY_ORDER + 1) * HY_CH
COL_KROPE = COL_CKV + MLA_KV_RANK
LANES = 128
IN_WIDTH_PAD = 4096

VMEM_LIMIT_BYTES = 56 * 1024 * 1024

NA_Q_ROWS = 8
NA_K_ROWS = 16


def _params(*sem):
    return pltpu.CompilerParams(dimension_semantics=sem, vmem_limit_bytes=VMEM_LIMIT_BYTES)


def _add_norm_kernel(*refs, n_delta, with_router, with_norm):
    x_ref = refs[0]
    d_refs = refs[1:1 + n_delta]
    pos = 1 + n_delta
    x = x_ref[...]
    for d in d_refs:
        x = x + d[...].astype(F32)
    if not with_norm:
        refs[pos][...] = x
        return
    g_ref = refs[pos]
    pos += 1
    if with_router:
        wr_ref = refs[pos]
        pos += 1
    xo_ref, h_ref = refs[pos], refs[pos + 1]
    xo_ref[...] = x
    h = x * lax.rsqrt(jnp.mean(x * x, axis=-1, keepdims=True) + EPS) * g_ref[...]
    h_ref[...] = h.astype(BF16)
    if with_router:
        refs[pos + 2][...] = jnp.dot(h, wr_ref[...], preferred_element_type=F32,
                                     precision=lax.Precision.HIGHEST)


def _add_norm(x, deltas, gain=None, router_w=None):
    T, D = x.shape
    tm = min(512, T)
    with_norm = gain is not None
    with_router = router_w is not None
    row = pl.BlockSpec((tm, D), lambda i: (i, 0))
    in_specs = [row] + [row] * len(deltas)
    args = [x] + list(deltas)
    out_shape = [jax.ShapeDtypeStruct((T, D), F32)]
    out_specs = [row]
    if with_norm:
        in_specs.append(pl.BlockSpec((1, D), lambda i: (0, 0)))
        args.append(gain.reshape(1, D))
        if with_router:
            in_specs.append(pl.BlockSpec((D, LANES), lambda i: (0, 0)))
            args.append(router_w)
        out_shape.append(jax.ShapeDtypeStruct((T, D), BF16))
        out_specs.append(row)
        if with_router:
            out_shape.append(jax.ShapeDtypeStruct((T, LANES), F32))
            out_specs.append(pl.BlockSpec((tm, LANES), lambda i: (i, 0)))
    out = pl.pallas_call(
        functools.partial(_add_norm_kernel, n_delta=len(deltas), with_router=with_router,
                          with_norm=with_norm),
        grid=(T // tm,), in_specs=in_specs, out_specs=out_specs, out_shape=out_shape,
        compiler_params=_params("parallel"), name="add_norm")(*args)
    return out


def _mm_kernel(a_ref, b_ref, o_ref):
    o_ref[...] = jnp.dot(a_ref[...], b_ref[...], preferred_element_type=F32).astype(o_ref.dtype)


def _matmul(a, b, out_dtype, tn):
    M, K = a.shape
    N = b.shape[1]
    tm = min(1024, M)
    return pl.pallas_call(
        _mm_kernel, grid=(M // tm, N // tn),
        in_specs=[pl.BlockSpec((tm, K), lambda i, j: (i, 0)),
                  pl.BlockSpec((K, tn), lambda i, j: (0, j))],
        out_specs=pl.BlockSpec((tm, tn), lambda i, j: (i, j)),
        out_shape=jax.ShapeDtypeStruct((M, N), out_dtype),
        compiler_params=_params("parallel", "parallel"), name="matmul")(a, b)


def _na_bias_table(rpb, rows):
    n_groups = rows // NA_Q_ROWS
    reps = (0, min(1, n_groups - 1), n_groups - 1)
    row_off = np.zeros((3, NA_Q_ROWS, NA_K_ROWS), np.int32)
    row_ok = np.zeros((3, NA_Q_ROWS, NA_K_ROWS), bool)
    for p, g in enumerate(reps):
        start = int(np.clip(g * NA_Q_ROWS - NA_WIN_ROWS // 2, 0, rows - NA_K_ROWS))
        for i in range(NA_Q_ROWS):
            r = g * NA_Q_ROWS + i
            rs = int(np.clip(r - NA_WIN_ROWS // 2, 0, rows - NA_WIN_ROWS))
            for j in range(NA_K_ROWS):
                kr = start + j
                ok = rs <= kr < rs + NA_WIN_ROWS
                row_ok[p, i, j] = ok
                row_off[p, i, j] = np.clip(kr - r + NA_WIN_ROWS - 1, 0, 2 * NA_WIN_ROWS - 2)
    cols = np.arange(GRID_W)
    col_start = np.clip(cols - NA_WIN_COLS // 2, 0, GRID_W - NA_WIN_COLS)
    col_ok = (cols[None, :] >= col_start[:, None]) & (cols[None, :] < col_start[:, None] + NA_WIN_COLS)
    col_off = np.clip(cols[None, :] - cols[:, None] + NA_WIN_COLS - 1, 0, 2 * NA_WIN_COLS - 2)
    row_sel = np.eye(2 * NA_WIN_ROWS - 1, dtype=np.float32)[row_off]
    col_sel = np.eye(2 * NA_WIN_COLS - 1, dtype=np.float32)[col_off]
    tab = jnp.einsum('hrc,pijr,abc->hpiajb', rpb.astype(F32), row_sel, col_sel,
                     precision=lax.Precision.HIGHEST)
    ok = row_ok[:, :, None, :, None] & col_ok[None, None, :, None, :]
    tab = jnp.where(ok[None], tab, NEG_INF).astype(BF16)
    return tab.reshape(NA_HEADS, 3, NA_Q_ROWS * GRID_W, NA_K_ROWS * GRID_W)


def _na_kernel(q_ref, k_ref, v_ref, bias_ref, gq_ref, gk_ref, ones_ref, o_ref, *, rows):
    g = pl.program_id(2)
    tq = NA_Q_ROWS * GRID_W
    tk = NA_K_ROWS * GRID_W
    start = jnp.clip(g * NA_Q_ROWS - NA_WIN_ROWS // 2, 0, rows - NA_K_ROWS) * GRID_W
    start = pl.multiple_of(start, GRID_W)
    ones_bd = ones_ref[...]

    def head_norm(x, gain):
        x2 = x * x
        hi = x2.astype(BF16)
        lo = (x2 - hi.astype(F32)).astype(BF16)
        ssq = (jnp.dot(hi, ones_bd, preferred_element_type=F32)
               + jnp.dot(lo, ones_bd, preferred_element_type=F32))
        return x * lax.rsqrt(ssq * (1.0 / NA_HEAD_DIM) + EPS) * gain

    q = head_norm(q_ref[0].astype(F32), gq_ref[...]) * (NA_HEAD_DIM ** -0.5)
    kw = head_norm(k_ref[0, pl.ds(start, tk), :].astype(F32), gk_ref[...]).astype(BF16)
    vw = v_ref[0, pl.ds(start, tk), :]
    lane = lax.broadcasted_iota(jnp.int32, (tq, LANES), 1)
    outs = []
    for hh in range(2):
        sel = (lane < NA_HEAD_DIM) if hh == 0 else (lane >= NA_HEAD_DIM)
        qm = jnp.where(sel, q, 0.0).astype(BF16)
        s = lax.dot_general(qm, kw, (((1,), (1,)), ((), ())), preferred_element_type=F32)
        s = s + bias_ref[hh, 0].astype(F32)
        m = jnp.max(s, axis=-1, keepdims=True)
        p = jnp.exp(s - m)
        l = jnp.sum(p, axis=-1, keepdims=True)
        o = jnp.dot(p.astype(BF16), vw, preferred_element_type=F32)
        outs.append(o / l)
    o_ref[0] = jnp.where(lane < NA_HEAD_DIM, outs[0], outs[1]).astype(o_ref.dtype)


def _neighborhood_attention(proj3, rpb, q_g, k_g):
    B, L, _ = proj3.shape
    rows = L // GRID_W
    n_groups = rows // NA_Q_ROWS
    tq = NA_Q_ROWS * GRID_W
    tk = NA_K_ROWS * GRID_W
    bias = _na_bias_table(rpb, rows)
    gq = jnp.tile(q_g.astype(F32), 2).reshape(1, LANES)
    gk = jnp.tile(k_g.astype(F32), 2).reshape(1, LANES)
    ones_bd = jnp.asarray(np.kron(np.eye(2), np.ones((NA_HEAD_DIM, NA_HEAD_DIM))), BF16)
    qb, kb, vb = COL_NA_Q // LANES, COL_NA_K // LANES, COL_NA_V // LANES

    def pattern(g):
        return (g > 0).astype(jnp.int32) + (g == n_groups - 1).astype(jnp.int32)

    return pl.pallas_call(
        functools.partial(_na_kernel, rows=rows),
        grid=(NA_HEADS // 2, B, n_groups),
        in_specs=[
            pl.BlockSpec((1, tq, LANES), lambda hp, b, g: (b, g, qb + hp)),
            pl.BlockSpec((1, L, LANES), lambda hp, b, g: (b, 0, kb + hp)),
            pl.BlockSpec((1, L, LANES), lambda hp, b, g: (b, 0, vb + hp)),
            pl.BlockSpec((2, 1, tq, tk), lambda hp, b, g: (hp, pattern(g), 0, 0)),
            pl.BlockSpec((1, LANES), lambda hp, b, g: (0, 0)),
            pl.BlockSpec((1, LANES), lambda hp, b, g: (0, 0)),
            pl.BlockSpec((LANES, LANES), lambda hp, b, g: (0, 0)),
        ],
        out_specs=pl.BlockSpec((1, tq, LANES), lambda hp, b, g: (b, g, hp)),
        out_shape=jax.ShapeDtypeStruct((B, L, NA_WIDTH), BF16),
        compiler_params=_params("parallel", "parallel", "arbitrary"),
        name="neighborhood_attention")(proj3, proj3, proj3, bias, gq, gk, ones_bd)


def _mla_prep_kernel(cq_ref, ckv_ref, krope_ref, gqn_ref, gkvn_ref, wuq_ref, wukv_ref,
                     qgn_ref, qgr_ref, kgn_ref, kgr_ref, cos_ref, sin_ref,
                     qn_ref, qr_ref, kn_ref, kr_ref, v_ref):
    def rms(x, g):
        return x * lax.rsqrt(jnp.mean(x * x, axis=-1, keepdims=True) + EPS) * g

    tm = cq_ref.shape[0]
    nope_w = MLA_HEADS * MLA_NOPE_DIM
    cq = rms(cq_ref[...].astype(F32), gqn_ref[...]).astype(BF16)
    q = jnp.dot(cq, wuq_ref[...], preferred_element_type=F32)
    ckv = rms(ckv_ref[...].astype(F32), gkvn_ref[...]).astype(BF16)
    kv = jnp.dot(ckv, wukv_ref[...], preferred_element_type=F32)
    v_ref[...] = kv[:, nope_w:].astype(BF16)

    lane = lax.broadcasted_iota(jnp.int32, (tm, LANES), 1)
    first = lane < MLA_ROPE_DIM
    low_half = (lane % MLA_ROPE_DIM) < (MLA_ROPE_DIM // 2)
    cosv = cos_ref[...]
    sinv = sin_ref[...]
    scale = (MLA_QK_DIM ** -0.5) * math.log2(math.e)
    inv_dim = 1.0 / MLA_QK_DIM

    def rope(x):
        swapped = jnp.where(low_half, pltpu.roll(x, LANES - MLA_ROPE_DIM // 2, 1),
                            pltpu.roll(x, MLA_ROPE_DIM // 2, 1))
        return x * cosv + swapped * sinv

    qgn, qgr = qgn_ref[...], qgr_ref[...]
    kgn, kgr = kgn_ref[...], kgr_ref[...]
    kraw = krope_ref[...].astype(F32)
    kdup = jnp.where(first, kraw, pltpu.roll(kraw, MLA_ROPE_DIM, 1))
    k_rope_sq = jnp.where(first, kraw * kraw, 0.0)
    k_roped = rope(kdup * kgr)

    for j in range(MLA_HEADS // 2):
        qr_j = q[:, nope_w + LANES * j:nope_w + LANES * (j + 1)]
        qr_sq = qr_j * qr_j
        q_rs, k_rs = [], []
        for hh in range(2):
            h = 2 * j + hh
            cols = slice(LANES * h, LANES * (h + 1))
            qn_h = q[:, cols]
            t = qn_h * qn_h + jnp.where(first if hh == 0 else jnp.logical_not(first), qr_sq, 0.0)
            rs = lax.rsqrt(jnp.sum(t, axis=-1, keepdims=True) * inv_dim + EPS)
            qn_ref[:, cols] = (qn_h * rs * qgn * scale).astype(BF16)
            q_rs.append(rs)
            kn_h = kv[:, cols]
            t = kn_h * kn_h + k_rope_sq
            rs = lax.rsqrt(jnp.sum(t, axis=-1, keepdims=True) * inv_dim + EPS)
            kn_ref[:, cols] = (kn_h * rs * kgn).astype(BF16)
            k_rs.append(rs)
        pair = slice(LANES * j, LANES * (j + 1))
        qr_ref[:, pair] = (rope(qr_j * jnp.where(first, q_rs[0], q_rs[1]) * qgr) * scale).astype(BF16)
        kr_ref[:, pair] = (k_roped * jnp.where(first, k_rs[0], k_rs[1])).astype(BF16)


def _mla_prep(proj, L, q_norm_g, kv_norm_g, w_uq, w_ukv, q_head_g, k_head_g):
    T = proj.shape[0]
    tm = min(512, L)
    H = MLA_HEADS
    wq = w_uq.reshape(MLA_Q_RANK, H, MLA_QK_DIM)
    wq = jnp.concatenate([wq[:, :, :MLA_NOPE_DIM].reshape(MLA_Q_RANK, -1),
                          wq[:, :, MLA_NOPE_DIM:].reshape(MLA_Q_RANK, -1)], axis=1).astype(BF16)
    wkv = w_ukv.reshape(MLA_KV_RANK, H, MLA_NOPE_DIM + MLA_V_DIM)
    wkv = jnp.concatenate([wkv[:, :, :MLA_NOPE_DIM].reshape(MLA_KV_RANK, -1),
                           wkv[:, :, MLA_NOPE_DIM:].reshape(MLA_KV_RANK, -1)], axis=1).astype(BF16)
    half = MLA_ROPE_DIM // 2
    freqs = ROPE_THETA ** (-jnp.arange(half, dtype=F32) / half)
    ang = jnp.arange(L).astype(F32)[:, None] * freqs[None, :]
    cos, sin = jnp.cos(ang), jnp.sin(ang)
    cos2 = jnp.tile(jnp.concatenate([cos, cos], axis=1), (1, 2))
    sin2 = jnp.tile(jnp.concatenate([-sin, sin], axis=1), (1, 2))

    def vec(g, reps=1):
        return jnp.tile(g.astype(F32), reps).reshape(1, -1)

    const = lambda shape: pl.BlockSpec(shape, lambda i: (0, 0))
    nblk = L // tm
    outs = pl.pallas_call(
        _mla_prep_kernel, grid=(T // tm,),
        in_specs=[
            pl.BlockSpec((tm, MLA_Q_RANK), lambda i: (i, COL_CQ // MLA_Q_RANK)),
            pl.BlockSpec((tm, MLA_KV_RANK), lambda i: (i, COL_CKV // MLA_KV_RANK)),
            pl.BlockSpec((tm, LANES), lambda i: (i, COL_KROPE // LANES)),
            const((1, MLA_Q_RANK)), const((1, MLA_KV_RANK)),
            const(wq.shape), const(wkv.shape),
            const((1, LANES)), const((1, LANES)), const((1, LANES)), const((1, LANES)),
            pl.BlockSpec((tm, LANES), lambda i: (i % nblk, 0)),
            pl.BlockSpec((tm, LANES), lambda i: (i % nblk, 0)),
        ],
        out_specs=[
            pl.BlockSpec((tm, H * MLA_NOPE_DIM), lambda i: (i, 0)),
            pl.BlockSpec((tm, H * MLA_ROPE_DIM), lambda i: (i, 0)),
            pl.BlockSpec((tm, H * MLA_NOPE_DIM), lambda i: (i, 0)),
            pl.BlockSpec((tm, H * MLA_ROPE_DIM), lambda i: (i, 0)),
            pl.BlockSpec((tm, H * MLA_V_DIM), lambda i: (i, 0)),
        ],
        out_shape=[
            jax.ShapeDtypeStruct((T, H * MLA_NOPE_DIM), BF16),
            jax.ShapeDtypeStruct((T, H * MLA_ROPE_DIM), BF16),
            jax.ShapeDtypeStruct((T, H * MLA_NOPE_DIM), BF16),
            jax.ShapeDtypeStruct((T, H * MLA_ROPE_DIM), BF16),
            jax.ShapeDtypeStruct((T, H * MLA_V_DIM), BF16),
        ],
        compiler_params=_params("parallel"), name="mla_prep")(
            proj, proj, proj, vec(q_norm_g), vec(kv_norm_g), wq, wkv,
            vec(q_head_g[:MLA_NOPE_DIM]), vec(q_head_g[MLA_NOPE_DIM:], 2),
            vec(k_head_g[:MLA_NOPE_DIM]), vec(k_head_g[MLA_NOPE_DIM:], 2), cos2, sin2)
    return outs


FLASH_STREAM_ROWS = 256


def _flash_kernel(qn_ref, qr_ref, kn0_ref, kr0_ref, kn1_ref, kr1_ref, v_ref, o_ref,
                  q_sc, s_even, s_odd, m_sc, acc_sc):
    h = pl.program_id(1)
    ki = pl.program_id(3)
    tq = qn_ref.shape[1]
    tk = kn0_ref.shape[1]
    rows = min(FLASH_STREAM_ROWS, tq)
    n_streams = tq // rows
    nt = (((1,), (1,)), ((), ()))

    @pl.when(ki == 0)
    def _():
        lane = lax.broadcasted_iota(jnp.int32, (tq, LANES), 1)
        own = (lane // MLA_ROPE_DIM) == (h % 2)
        q_sc[:, :LANES] = qn_ref[0]
        q_sc[:, LANES:] = jnp.where(own, qr_ref[0], jnp.zeros_like(qr_ref[0]))
        m_sc[...] = jnp.full_like(m_sc, NEG_INF)
        acc_sc[...] = jnp.zeros_like(acc_sc)
        k0 = jnp.concatenate([kn0_ref[0], kr0_ref[0]], axis=-1)
        for r in range(n_streams):
            sl = slice(r * rows, (r + 1) * rows)
            s_even[sl, :] = lax.dot_general(q_sc[sl, :], k0, nt, preferred_element_type=F32)

    def step(s_cur, s_next):
        k1 = jnp.concatenate([kn1_ref[0], kr1_ref[0]], axis=-1)
        v_ones = jnp.concatenate([v_ref[0], jnp.ones((tk, LANES), BF16)], axis=-1)
        for r in range(n_streams):
            sl = slice(r * rows, (r + 1) * rows)
            s_next[sl, :] = lax.dot_general(q_sc[sl, :], k1, nt, preferred_element_type=F32)
            s = s_cur[sl, :]
            m_prev = m_sc[sl, :]
            m_new = jnp.maximum(m_prev, jnp.max(s, axis=-1, keepdims=True))
            p = jnp.exp2(s - m_new).astype(BF16)
            acc_sc[sl, :] = (jnp.exp2(m_prev - m_new) * acc_sc[sl, :]
                             + jnp.dot(p, v_ones, preferred_element_type=F32))
            m_sc[sl, :] = m_new

    @pl.when(ki % 2 == 0)
    def _():
        step(s_even, s_odd)

    @pl.when(ki % 2 == 1)
    def _():
        step(s_odd, s_even)

    @pl.when(ki == pl.num_programs(3) - 1)
    def _():
        o_ref[0] = (acc_sc[:, :LANES] / acc_sc[:, LANES:]).astype(o_ref.dtype)


def _flash_attention(qn, qr, kn, kr, v, B, L):
    H = MLA_HEADS
    tq = min(2048, L)
    tk = min(1024, L)
    r3 = lambda a: a.reshape(B, L, a.shape[-1])
    nk = L // tk
    ahead = lambda ki: jnp.minimum(ki + 1, nk - 1)
    return pl.pallas_call(
        _flash_kernel, grid=(B, H, L // tq, nk),
        in_specs=[
            pl.BlockSpec((1, tq, LANES), lambda b, h, qi, ki: (b, qi, h)),
            pl.BlockSpec((1, tq, LANES), lambda b, h, qi, ki: (b, qi, h // 2)),
            pl.BlockSpec((1, tk, LANES), lambda b, h, qi, ki: (b, 0, h)),
            pl.BlockSpec((1, tk, LANES), lambda b, h, qi, ki: (b, 0, h // 2)),
            pl.BlockSpec((1, tk, LANES), lambda b, h, qi, ki: (b, ahead(ki), h)),
            pl.BlockSpec((1, tk, LANES), lambda b, h, qi, ki: (b, ahead(ki), h // 2)),
            pl.BlockSpec((1, tk, LANES), lambda b, h, qi, ki: (b, ki, h)),
        ],
        out_specs=pl.BlockSpec((1, tq, LANES), lambda b, h, qi, ki: (b, qi, h)),
        out_shape=jax.ShapeDtypeStruct((B, L, H * MLA_V_DIM), BF16),
        scratch_shapes=[pltpu.VMEM((tq, 2 * LANES), BF16), pltpu.VMEM((tq, tk), F32),
                        pltpu.VMEM((tq, tk), F32), pltpu.VMEM((tq, 1), F32),
                        pltpu.VMEM((tq, 2 * LANES), F32)],
        compiler_params=_params("parallel", "parallel", "parallel", "arbitrary"),
        name="mla_flash")(r3(qn), r3(qr), r3(kn), r3(kr), r3(kn), r3(kr), r3(v))


def _hyena_filters(L, w1, b1, w2, b2, w3, freq):
    t = jnp.linspace(0.0, 1.0, L, dtype=F32)[:, None]
    bands = jnp.linspace(1e-4, HY_POS_BANDS - 1, HY_POS_BANDS, dtype=F32)[None, :]
    w = 2.0 * math.pi * jnp.arange(L, dtype=F32)[:, None] / L
    z = jnp.concatenate([t, jnp.cos(bands * w), -jnp.sin(bands * w)], axis=-1)
    hp = lax.Precision.HIGHEST
    deltas = jnp.abs(jnp.linspace(HY_MIN_DECAY, HY_MAX_DECAY, HY_CH, dtype=F32))
    w3d = w3.astype(F32).reshape(w3.shape[0], 2, HY_ORDER * HY_CH)

    def direction(z, t, d):
        a = jnp.sin(freq[0] * (jnp.matmul(z, w1, precision=hp) + b1))
        a = jnp.sin(freq[1] * (jnp.matmul(a, w2, precision=hp) + b2))
        f = jnp.matmul(a, w3d[:, d], precision=hp).reshape(L, HY_ORDER, HY_CH)
        return f * jnp.exp(-t * deltas[None, :])[:, None, :]

    fwd = direction(z, t, 0)
    bwd_rev = direction(z[::-1], t[::-1], 1)[:L - 1]
    k = jnp.concatenate([fwd, jnp.zeros((1, HY_ORDER, HY_CH), F32), bwd_rev], axis=0)
    return k * lax.rsqrt(jnp.sum(k * k, axis=0, keepdims=True) + EPS)


HALO_ROWS = 16


def _short_conv_kernel(x_ref, xp_ref, xn_ref, w_ref, b_ref, o_ref, *, tiles_per_seq):
    i = pl.program_id(0)
    x = x_ref[...].astype(F32)
    tm = x.shape[0]
    row = lax.broadcasted_iota(jnp.int32, x.shape, 0)
    t = i % tiles_per_seq
    halo_prev = jnp.where(t == 0, 0.0, xp_ref[HALO_ROWS - 1:HALO_ROWS, :].astype(F32))
    halo_next = jnp.where(t == tiles_per_seq - 1, 0.0, xn_ref[0:1, :].astype(F32))
    prev = jnp.where(row == 0, halo_prev, pltpu.roll(x, 1, 0))
    nxt = jnp.where(row == tm - 1, halo_next, pltpu.roll(x, tm - 1, 0))
    o_ref[...] = prev * w_ref[0:1, :] + x * w_ref[1:2, :] + nxt * w_ref[2:3, :] + b_ref[...]


def _short_conv(proj, L, w, b):
    T = proj.shape[0]
    W = (HY_ORDER + 1) * HY_CH
    tm = min(1024, L)
    tc = 512
    c0 = COL_HY // tc
    hb = tm // HALO_ROWS
    n_halo = T // HALO_ROWS
    return pl.pallas_call(
        functools.partial(_short_conv_kernel, tiles_per_seq=L // tm),
        grid=(T // tm, W // tc),
        in_specs=[
            pl.BlockSpec((tm, tc), lambda i, c: (i, c0 + c)),
            pl.BlockSpec((HALO_ROWS, tc), lambda i, c: (jnp.maximum(i * hb - 1, 0), c0 + c)),
            pl.BlockSpec((HALO_ROWS, tc), lambda i, c: (jnp.minimum((i + 1) * hb, n_halo - 1), c0 + c)),
            pl.BlockSpec((3, tc), lambda i, c: (0, c)),
            pl.BlockSpec((1, tc), lambda i, c: (0, c)),
        ],
        out_specs=pl.BlockSpec((tm, tc), lambda i, c: (i, c)),
        out_shape=jax.ShapeDtypeStruct((T, W), F32),
        compiler_params=_params("parallel", "parallel"), name="hyena_short_conv")(
            proj, proj, proj, w.astype(F32), b.astype(F32).reshape(1, W))


HY_CB = LANES
HY_NLO = 128
HY_UNROLL = 4
SUBLANES = 8


def _dft_geometry(L):
    nhi = 2 * L // HY_NLO
    n_kb = nhi // 2 + 1
    kb_pad = -(-n_kb // SUBLANES) * SUBLANES
    chunk = max(d for d in range(1, 17) if n_kb % d == 0)
    return nhi, n_kb, kb_pad, chunk


def _dft_tables(L):
    N = 2 * L
    nhi, n_kb, kb_pad, _ = _dft_geometry(L)
    kb = jnp.arange(kb_pad, dtype=jnp.int32)
    nh = jnp.arange(nhi, dtype=jnp.int32)
    ang = (2.0 * math.pi / nhi) * ((kb[:, None] * nh[None, :]) % nhi).astype(F32)
    cos, sin = jnp.cos(ang), jnp.sin(ang)
    f1 = jnp.concatenate([cos, -sin], axis=0)
    wgt = jnp.where((kb == 0) | (kb == nhi // 2), 1.0, jnp.where(kb < n_kb, 2.0, 0.0))[:, None]
    f1_inv = jnp.concatenate([cos * wgt, -sin * wgt], axis=0)[:, :nhi // 2].T
    ka = jnp.arange(HY_NLO, dtype=jnp.int32)
    idx = (ka[None, None, :] * (ka[None, :, None] * nhi + kb[:n_kb, None, None])) % N
    ang = (2.0 * math.pi / N) * idx.astype(F32)
    gr, gi = jnp.cos(ang), -jnp.sin(ang)
    g = jnp.concatenate([jnp.concatenate([gr, -gi], axis=2),
                         jnp.concatenate([gi, gr], axis=2)], axis=1)
    return f1.astype(BF16), f1_inv.astype(BF16), g.astype(BF16)


def _dft_major_stage(src_ref, f1, a_ref, n_rows):
    two_nhi = f1.shape[0]

    def body(n_lo, carry):
        xs = src_ref[pl.ds(n_lo, n_rows, stride=HY_NLO), :]
        dst = pl.multiple_of(n_lo * two_nhi, two_nhi)
        a_ref[pl.ds(dst, two_nhi), :] = jnp.dot(f1, xs.astype(BF16), preferred_element_type=F32)
        return carry

    lax.fori_loop(0, HY_NLO, body, 0, unroll=HY_UNROLL)


def _load_slab(a_ref, kb, kb_pad):
    re = a_ref[pl.ds(kb, HY_NLO, stride=2 * kb_pad), :]
    im = a_ref[pl.ds(kb_pad + kb, HY_NLO, stride=2 * kb_pad), :]
    return jnp.concatenate([re, im], axis=0).astype(BF16)


def _hyena_spectrum_kernel(k_ref, f1_ref, g_ref, o_ref, a_ref, *, nhi, kb_pad, chunk):
    j = pl.program_id(1)

    @pl.when(j == 0)
    def _():
        _dft_major_stage(k_ref, f1_ref[...], a_ref, nhi)

    def slab(s, carry):
        kb = j * chunk + s
        o_ref[0, s] = jnp.dot(g_ref[s], _load_slab(a_ref, kb, kb_pad), preferred_element_type=F32)
        return carry

    lax.fori_loop(0, chunk, slab, 0, unroll=chunk)


def _hyena_spectrum(k2, f1, g):
    N, n_ch = k2.shape
    nhi, n_kb, kb_pad, chunk = _dft_geometry(N // 2)
    return pl.pallas_call(
        functools.partial(_hyena_spectrum_kernel, nhi=nhi, kb_pad=kb_pad, chunk=chunk),
        grid=(n_ch // HY_CB, n_kb // chunk),
        in_specs=[
            pl.BlockSpec((N, HY_CB), lambda c, j: (0, c)),
            pl.BlockSpec(f1.shape, lambda c, j: (0, 0)),
            pl.BlockSpec((chunk, 2 * HY_NLO, 2 * HY_NLO), lambda c, j: (j, 0, 0)),
        ],
        out_specs=pl.BlockSpec((1, chunk, 2 * HY_NLO, HY_CB), lambda c, j: (c, j, 0, 0)),
        out_shape=jax.ShapeDtypeStruct((n_ch // HY_CB, n_kb, 2 * HY_NLO, HY_CB), F32),
        scratch_shapes=[pltpu.VMEM((HY_NLO * 2 * kb_pad, HY_CB), F32)],
        compiler_params=_params("parallel", "arbitrary"), name="hyena_spectrum")(k2, f1, g)


def _hyena_conv_kernel(u_ref, gate_ref, bias_ref, f1_ref, f1t_ref, g_ref, kf_ref, o_ref,
                       a_ref, y_ref, *, n_in, kb_pad, chunk):
    j = pl.program_id(2)
    L = n_in * HY_NLO

    @pl.when(j == 0)
    def _():
        _dft_major_stage(u_ref.at[0], f1_ref[...], a_ref, n_in)

    def forward(s, carry):
        kb = j * chunk + s
        t = jnp.dot(g_ref[s], _load_slab(a_ref, kb, kb_pad), preferred_element_type=F32)
        kf = kf_ref[0, s]
        tr, ti = t[:HY_NLO], t[HY_NLO:]
        kr, ki = kf[:HY_NLO], kf[HY_NLO:]
        y_ref[s] = jnp.concatenate([tr * kr - ti * ki, tr * ki + ti * kr], axis=0).astype(BF16)
        return carry

    lax.fori_loop(0, chunk, forward, 0, unroll=chunk)

    def inverse(s, carry):
        kb = j * chunk + s
        r = lax.dot_general(g_ref[s], y_ref[s], (((0,), (0,)), ((), ())),
                            preferred_element_type=F32)
        a_ref[pl.ds(kb, HY_NLO, stride=2 * kb_pad), :] = r[:HY_NLO]
        a_ref[pl.ds(kb_pad + kb, HY_NLO, stride=2 * kb_pad), :] = r[HY_NLO:]
        return carry

    lax.fori_loop(0, chunk, inverse, 0, unroll=chunk)

    @pl.when(j == pl.num_programs(2) - 1)
    def _():
        f1t = f1t_ref[...]

        def body(n_lo, carry):
            src = pl.multiple_of(n_lo * 2 * kb_pad, 2 * kb_pad)
            blk = a_ref[pl.ds(src, 2 * kb_pad), :].astype(BF16)
            o_ref[0, pl.ds(n_lo, n_in, stride=HY_NLO), :] = jnp.dot(
                f1t, blk, preferred_element_type=F32)
            return carry

        lax.fori_loop(0, HY_NLO, body, 0, unroll=HY_UNROLL)
        inv_n = 1.0 / (2 * L)
        rows = 512

        def gate_rows(c, carry):
            r0 = pl.multiple_of(c * rows, rows)
            sl = pl.ds(r0, rows)
            o_ref[0, sl, :] = gate_ref[0, sl, :] * (o_ref[0, sl, :] * inv_n
                                                    + u_ref[0, sl, :] * bias_ref[...])
            return carry

        lax.fori_loop(0, L // rows, gate_rows, 0)


def _hyena_conv(u3, u_col, gate3, gate_col, bias, kf, kf_row, f1, f1_inv, g):
    B, L, _ = u3.shape
    nhi, n_kb, kb_pad, chunk = _dft_geometry(L)
    n_in = nhi // 2
    f1_in = f1[:, :n_in]
    n_cb = HY_CH // HY_CB
    return pl.pallas_call(
        functools.partial(_hyena_conv_kernel, n_in=n_in, kb_pad=kb_pad, chunk=chunk),
        grid=(B, n_cb, n_kb // chunk),
        in_specs=[
            pl.BlockSpec((1, L, HY_CB), lambda b, c, j: (b, 0, u_col + c)),
            pl.BlockSpec((1, L, HY_CB), lambda b, c, j: (b, 0, gate_col + c)),
            pl.BlockSpec((1, HY_CB), lambda b, c, j: (0, c)),
            pl.BlockSpec(f1_in.shape, lambda b, c, j: (0, 0)),
            pl.BlockSpec(f1_inv.shape, lambda b, c, j: (0, 0)),
            pl.BlockSpec((chunk, 2 * HY_NLO, 2 * HY_NLO), lambda b, c, j: (j, 0, 0)),
            pl.BlockSpec((1, chunk, 2 * HY_NLO, HY_CB), lambda b, c, j: (kf_row + c, j, 0, 0)),
        ],
        out_specs=pl.BlockSpec((1, L, HY_CB), lambda b, c, j: (b, 0, c)),
        out_shape=jax.ShapeDtypeStruct((B, L, HY_CH), F32),
        scratch_shapes=[pltpu.VMEM((HY_NLO * 2 * kb_pad, HY_CB), F32),
                        pltpu.VMEM((chunk, 2 * HY_NLO, HY_CB), BF16)],
        compiler_params=_params("parallel", "parallel", "arbitrary"), name="hyena_long_conv")(
            u3, gate3, bias.astype(F32).reshape(1, HY_CH), f1_in, f1_inv, g, kf)


def _hyena(proj, B, L, short_w, short_b, w1, b1, w2, b2, w3, freq, bias, dft):
    f1, f1_inv, g = dft
    u = _short_conv(proj, L, short_w, short_b).reshape(B, L, (HY_ORDER + 1) * HY_CH)
    k = _hyena_filters(L, w1, b1, w2, b2, w3, freq).reshape(2 * L, HY_ORDER * HY_CH)
    kf = _hyena_spectrum(k, f1, g)
    n_cb = HY_CH // HY_CB
    z = _hyena_conv(u, 0, u, n_cb, bias[0], kf, 0, f1, f1_inv, g)
    return _hyena_conv(z, 0, u, 2 * n_cb, bias[1], kf, n_cb, f1, f1_inv, g)


def _mix_norm_kernel(na_ref, mla_ref, hy_ref, g_ref, o_ref):
    def rms(x, g):
        return (x * lax.rsqrt(jnp.mean(x * x, axis=-1, keepdims=True) + EPS) * g).astype(BF16)

    a, b = NA_WIDTH, NA_WIDTH + MLA_HEADS * MLA_V_DIM
    o_ref[:, :a] = rms(na_ref[...].astype(F32), g_ref[:, :a])
    o_ref[:, a:b] = rms(mla_ref[...].astype(F32), g_ref[:, a:b])
    o_ref[:, b:] = rms(hy_ref[...].astype(F32), g_ref[:, b:])


def _mix_norm(o_na, o_mla, o_hy, gain):
    T = o_na.shape[0]
    W = o_na.shape[1] + o_mla.shape[1] + o_hy.shape[1]
    tm = min(1024, T)
    row = lambda a: pl.BlockSpec((tm, a.shape[1]), lambda i: (i, 0))
    return pl.pallas_call(
        _mix_norm_kernel, grid=(T // tm,),
        in_specs=[row(o_na), row(o_mla), row(o_hy), pl.BlockSpec((1, W), lambda i: (0, 0))],
        out_specs=pl.BlockSpec((tm, W), lambda i: (i, 0)),
        out_shape=jax.ShapeDtypeStruct((T, W), BF16),
        compiler_params=_params("parallel"), name="mix_norm")(
            o_na, o_mla, o_hy, gain.astype(F32).reshape(1, W))


def _ffn_kernel(te_ref, na_ref, h_ref, wg_ref, wu_ref, wd_ref, rw_ref, o_ref, acc_ref):
    i = pl.program_id(0)
    j = pl.program_id(1)
    last = pl.num_programs(1) - 1
    active = i < na_ref[0]

    @pl.when(jnp.logical_and(active, j == 0))
    def _():
        acc_ref[...] = jnp.zeros_like(acc_ref)

    @pl.when(active)
    def _():
        h = h_ref[...]
        g = jnp.dot(h, wg_ref[0], preferred_element_type=F32)
        u = jnp.dot(h, wu_ref[0], preferred_element_type=F32)
        a = (g * (1.0 / (1.0 + jnp.exp(-g))) * u).astype(BF16)
        acc_ref[...] += jnp.dot(a, wd_ref[0], preferred_element_type=F32)

    @pl.when(jnp.logical_and(active, j == last))
    def _():
        o_ref[...] = (acc_ref[...] * rw_ref[...]).astype(o_ref.dtype)

    @pl.when(jnp.logical_and(jnp.logical_not(active), j == last))
    def _():
        o_ref[...] = jnp.zeros_like(o_ref)


def _ffn(h, w_gate, w_up, w_down, tile_expert, n_active, row_weight, tm):
    P, D = h.shape
    F = w_gate.shape[2]
    tf = 512
    nf = F // tf

    def fidx(i, j, na):
        return jnp.where(i < na[0], j, nf - 1)

    grid_spec = pltpu.PrefetchScalarGridSpec(
        num_scalar_prefetch=2, grid=(P // tm, nf),
        in_specs=[
            pl.BlockSpec((tm, D), lambda i, j, te, na: (i, 0)),
            pl.BlockSpec((1, D, tf), lambda i, j, te, na: (te[i], 0, fidx(i, j, na))),
            pl.BlockSpec((1, D, tf), lambda i, j, te, na: (te[i], 0, fidx(i, j, na))),
            pl.BlockSpec((1, tf, D), lambda i, j, te, na: (te[i], fidx(i, j, na), 0)),
            pl.BlockSpec((tm, 1), lambda i, j, te, na: (i, 0)),
        ],
        out_specs=pl.BlockSpec((tm, D), lambda i, j, te, na: (i, 0)),
        scratch_shapes=[pltpu.VMEM((tm, D), F32)])
    return pl.pallas_call(
        _ffn_kernel, grid_spec=grid_spec,
        out_shape=jax.ShapeDtypeStruct((P, D), BF16),
        compiler_params=_params("parallel", "arbitrary"), name="swiglu_ffn")(
            tile_expert, n_active, h, w_gate, w_up, w_down, row_weight)


def _dense_ffn(h, w_gate, w_up, w_down):
    T = h.shape[0]
    tm = min(1024, T)
    n = T // tm
    return _ffn(h, w_gate[None].astype(BF16), w_up[None].astype(BF16), w_down[None].astype(BF16),
                jnp.zeros((n,), jnp.int32), jnp.full((1,), n, jnp.int32),
                jnp.ones((T, 1), F32), tm)


def _moe_ffn(h, logits, w_gate, w_up, w_down):
    T, D = h.shape
    E = N_EXPERTS
    tm = min(1024, T)
    probs = jax.nn.softmax(logits[:, :E], axis=-1)
    top_p, top_i = lax.top_k(probs, TOP_K)
    top_p = top_p / jnp.sum(top_p, axis=-1, keepdims=True)
    flat_e = top_i.reshape(-1).astype(jnp.int32)
    n_slots = T * TOP_K
    order = jnp.argsort(flat_e, stable=True).astype(jnp.int32)
    counts = jnp.sum(flat_e[:, None] == jnp.arange(E, dtype=jnp.int32)[None, :], axis=0).astype(jnp.int32)
    tiles_per = (counts + tm - 1) // tm
    tile_end = jnp.cumsum(tiles_per)
    row_start = (tile_end - tiles_per) * tm
    slot_start = jnp.cumsum(counts) - counts
    n_tiles = n_slots // tm + E
    P = n_tiles * tm
    n_active = tile_end[-1:].astype(jnp.int32)
    tile_ids = jnp.arange(n_tiles, dtype=jnp.int32)
    tile_expert = jnp.sum(tile_ids[:, None] >= tile_end[None, :], axis=1).astype(jnp.int32)
    tile_expert = jnp.minimum(tile_expert, tile_expert[jnp.maximum(n_active[0] - 1, 0)])
    row_e = jnp.repeat(tile_expert, tm)
    rank = jnp.arange(P, dtype=jnp.int32) - row_start[row_e]
    row_ok = jnp.logical_and(jnp.repeat(tile_ids, tm) < n_active[0], rank < counts[row_e])
    row_slot = order[jnp.clip(slot_start[row_e] + rank, 0, n_slots - 1)]
    src_token = jnp.where(row_ok, row_slot // TOP_K, 0)
    row_weight = jnp.where(row_ok, top_p.reshape(-1)[row_slot], 0.0)
    sorted_pos = jnp.argsort(order).astype(jnp.int32)
    dest = row_start[flat_e] + sorted_pos - slot_start[flat_e]
    hs = jnp.take(h, src_token, axis=0)
    y = _ffn(hs, w_gate.astype(BF16), w_up.astype(BF16), w_down.astype(BF16),
             tile_expert, n_active, row_weight.reshape(P, 1), tm)
    dest = dest.reshape(T, TOP_K)
    return jnp.take(y, dest[:, 0], axis=0), jnp.take(y, dest[:, 1], axis=0)


def kernel(x, attn_norm_g, w_in, na_q_g, na_k_g, na_rpb, mla_q_norm_g, mla_kv_norm_g, mla_w_uq, mla_w_ukv, mla_q_g, mla_k_g, hy_short_w, hy_short_b, hy_w1, hy_b1, hy_w2, hy_b2, hy_w3, hy_freq, hy_bias, group_norm_g, w_out, ffn_norm_g, dense_w_gate, dense_w_up, dense_w_down, router_w, moe_w_gate, moe_w_up, moe_w_down):
    B, L, D = x.shape
    T = B * L
    depth = attn_norm_g.shape[0]
    assert L % (NA_K_ROWS * GRID_W) == 0 and w_in.shape[2] == IN_WIDTH
    x2 = x.reshape(T, D).astype(F32)
    dft = _dft_tables(L)
    deltas = []
    for l in range(depth):
        x2, h = _add_norm(x2, deltas, attn_norm_g[l].astype(F32))
        w = w_in[l]
        w_in_p = jnp.concatenate(
            [w[:, :SRC_CKV], w[:, SRC_HY:], w[:, SRC_CKV:SRC_HY],
             jnp.zeros((D, IN_WIDTH_PAD - IN_WIDTH), w.dtype)], axis=1).astype(BF16)
        proj = _matmul(h, w_in_p, BF16, 1024)
        o_na = _neighborhood_attention(proj.reshape(B, L, IN_WIDTH_PAD), na_rpb[l],
                                       na_q_g[l], na_k_g[l]).reshape(T, NA_WIDTH)
        qn, qr, kn, kr, v = _mla_prep(proj, L, mla_q_norm_g[l], mla_kv_norm_g[l], mla_w_uq[l],
                                      mla_w_ukv[l], mla_q_g[l], mla_k_g[l])
        o_mla = _flash_attention(qn, qr, kn, kr, v, B, L).reshape(T, MLA_HEADS * MLA_V_DIM)
        o_hy = _hyena(proj, B, L, hy_short_w[l], hy_short_b[l], hy_w1[l], hy_b1[l], hy_w2[l],
                      hy_b2[l], hy_w3[l], hy_freq[l], hy_bias[l], dft).reshape(T, HY_CH)
        mix = _mix_norm(o_na, o_mla, o_hy, group_norm_g[l])
        d_mix = _matmul(mix, w_out[l].astype(BF16), BF16, 1024)
        i = l // 2
        if l % 2 == 0:
            x2, h = _add_norm(x2, [d_mix], ffn_norm_g[l].astype(F32))
            deltas = [_dense_ffn(h, dense_w_gate[i], dense_w_up[i], dense_w_down[i])]
        else:
            wr = jnp.pad(router_w[i].astype(F32), ((0, 0), (0, LANES - N_EXPERTS)))
            x2, h, logits = _add_norm(x2, [d_mix], ffn_norm_g[l].astype(F32), wr)
            deltas = list(_moe_ffn(h, logits, moe_w_gate[i], moe_w_up[i], moe_w_down[i]))
    (x2,) = _add_norm(x2, deltas)
    return x2.reshape(B, L, D).astype(x.dtype)
```
